```python
import math
import jax
import jax.numpy as jnp
from jax import lax
import numpy as np

D_MODEL = 2048
BATCH = 1
SEQ = 8192
DEPTH = 2

GRID_W = 64
CTX_LEN = 256
EPS = 1e-6
F32 = jnp.float32

GROUP_W = D_MODEL // 4
MIX_W = 4 * GROUP_W
RET_HEADS = 4
RET_DIM = GROUP_W // RET_HEADS
RET_CHUNK = 128
RET_COLS = 5 * GROUP_W
S5_CH = GROUP_W
S5_GROUP = 16
S5_NG = S5_CH // S5_GROUP
S5_P = 64
S5_COLS = GROUP_W
ATT_HEADS = 4
ATT_KV_HEADS = 2
ATT_DIM = GROUP_W // ATT_HEADS
ATT_COLS = (ATT_HEADS + 2 * ATT_KV_HEADS) * ATT_DIM
Q_BLOCK = 128
ROPE_THETA = 10000.0
CONV_CH = GROUP_W
CONV_W = 3
CONV_COLS = 3 * GROUP_W
IN_COLS = RET_COLS + S5_COLS + ATT_COLS + CONV_COLS
N_EXPERTS = 32
TOP_K = 4
D_EXPERT = D_MODEL // 2
SWIGLU_LIMIT = 7.0
SWIGLU_ALPHA = 1.702
MOE_BLOCK = 128

kernel_name = 'hybrid_parallel_group_flow_block'


def rms_norm(x):
    xf = x.astype(F32)
    return (xf * lax.rsqrt(jnp.mean(xf * xf, -1, keepdims=True) + EPS)).astype(x.dtype)


def modulate(x, shift, scale):
    return rms_norm(x) * (1.0 + scale) + shift


def head_rms(x, w):
    xf = x.astype(F32)
    return xf * lax.rsqrt(jnp.mean(xf * xf, -1, keepdims=True) + EPS) * w.astype(F32)


def group_norm(y):
    yc = y - jnp.mean(y, -1, keepdims=True)
    return yc * lax.rsqrt(jnp.mean(yc * yc, -1, keepdims=True) + EPS)


def rope_angles(pos, dim):
    freqs = ROPE_THETA ** (-jnp.arange(0, dim, 2, dtype=F32) / dim)
    return pos.astype(F32)[:, None] * freqs[None, :]


def apply_rope(x, ang):
    cos = jnp.cos(ang)[None, :, None, :]
    sin = jnp.sin(ang)[None, :, None, :]
    x1, x2 = jnp.split(x, 2, axis=-1)
    return jnp.concatenate([x1 * cos - x2 * sin, x1 * sin + x2 * cos], axis=-1)


def apply_axial_rope(x, rows, cols):
    half = x.shape[-1] // 2
    xr, xc = jnp.split(x, 2, axis=-1)
    return jnp.concatenate([apply_rope(xr, rope_angles(rows, half)),
                            apply_rope(xc, rope_angles(cols, half))], axis=-1)


def retention_chunkwise(q, k, v, log_g, s0, with_output):
    bsz, t, h, dk = q.shape
    dv = v.shape[-1]
    n = t // RET_CHUNK
    qc = q.reshape(bsz, n, RET_CHUNK, h, dk)
    kc = k.reshape(bsz, n, RET_CHUNK, h, dk)
    vc = v.reshape(bsz, n, RET_CHUNK, h, dv)
    idx = jnp.arange(RET_CHUNK, dtype=F32)
    zeta = jnp.exp((RET_CHUNK - 1.0 - idx)[:, None] * log_g[None, :])
    kv = jnp.einsum('bnjhd,bnjhe->bnhde', kc * zeta[None, None, :, :, None], vc)
    g_chunk = jnp.exp(RET_CHUNK * log_g)[None, :, None, None]

    def step(s, kv_i):
        return s * g_chunk + kv_i, (s if with_output else None)

    s_final, s_prev = lax.scan(step, s0, jnp.moveaxis(kv, 1, 0))
    if not with_output:
        return None, s_final
    s_prev = jnp.moveaxis(s_prev, 0, 1)
    diff = idx[:, None] - idx[None, :]
    decay = jnp.where(diff >= 0, jnp.exp(jnp.maximum(diff, 0.0)[None] * log_g[:, None, None]), 0.0)
    scores = jnp.einsum('bnihd,bnjhd->bnhij', qc, kc) * decay[None, None]
    xi = jnp.exp((idx + 1.0)[:, None] * log_g[None, :])
    y = (jnp.einsum('bnhij,bnjhe->bnihe', scores, vc)
         + jnp.einsum('bnihd,bnhde->bnihe', qc, s_prev) * xi[None, None, :, :, None])
    return y.reshape(bsz, t, h, dv), s_final


def retention_mixer(z_lat, z_ctx, pos_lat, ctx_out):
    def heads(z):
        q, k, v, gf, gb = jnp.split(z.astype(F32), 5, axis=-1)
        hd = lambda a: a.reshape(a.shape[0], a.shape[1], RET_HEADS, RET_DIM)
        return hd(q), hd(k) * RET_DIM ** -0.5, hd(v), gf, gb

    ql, kl, vl, gfl, gbl = heads(z_lat)
    qc, kc, vc, gfc, gbc = heads(z_ctx)
    ang = rope_angles(pos_lat, RET_DIM)
    ql, kl = apply_rope(ql, ang), apply_rope(kl, ang)
    log_g = jnp.log(1.0 - 2.0 ** (-5.0 - jnp.arange(RET_HEADS, dtype=F32)))
    s0 = jnp.zeros((z_lat.shape[0], RET_HEADS, RET_DIM, RET_DIM), F32)
    y_lat, y_ctx = [], []
    for dr, lg in enumerate((log_g, log_g[::-1])):
        flip = (lambda a: a[:, ::-1]) if dr == 1 else (lambda a: a)
        yc, s_ctx = retention_chunkwise(flip(qc), flip(kc), flip(vc), lg, s0, ctx_out)
        yl, _ = retention_chunkwise(flip(ql), flip(kl), flip(vl), lg, s_ctx, True)
        y_lat.append(group_norm(flip(yl)).reshape(z_lat.shape[0], z_lat.shape[1], GROUP_W))
        if ctx_out:
            y_ctx.append(group_norm(flip(yc)).reshape(z_ctx.shape[0], z_ctx.shape[1], GROUP_W))
    out_lat = jax.nn.silu(gfl) * y_lat[0] + jax.nn.silu(gbl) * y_lat[1]
    out_ctx = (jax.nn.silu(gfc) * y_ctx[0] + jax.nn.silu(gbc) * y_ctx[1]).astype(z_ctx.dtype) if ctx_out else None
    return out_lat.astype(z_lat.dtype), out_ctx


def s5_discretize(a_re, a_im, log_step):
    a_re = a_re.astype(F32)
    a_im = a_im.astype(F32)
    step = jnp.exp(log_step.astype(F32))[:, None]
    mag = jnp.exp(a_re * step)
    ab_re = mag * jnp.cos(a_im * step)
    ab_im = mag * jnp.sin(a_im * step)
    den = a_re * a_re + a_im * a_im
    num_re = ab_re - 1.0
    f_re = (num_re * a_re + ab_im * a_im) / den
    f_im = (ab_im * a_re - num_re * a_im) / den
    return ab_re, ab_im, f_re, f_im


def s5_scan(drive_re, drive_im, ab_re, ab_im, s0_re, s0_im):
    drive_re = drive_re.at[:, 0].add(ab_re * s0_re - ab_im * s0_im)
    drive_im = drive_im.at[:, 0].add(ab_re * s0_im + ab_im * s0_re)
    a_re = jnp.broadcast_to(ab_re, drive_re.shape)
    a_im = jnp.broadcast_to(ab_im, drive_im.shape)

    def combine(e1, e2):
        a1r, a1i, b1r, b1i = e1
        a2r, a2i, b2r, b2i = e2
        return (a2r * a1r - a2i * a1i, a2r * a1i + a2i * a1r,
                a2r * b1r - a2i * b1i + b2r, a2r * b1i + a2i * b1r + b2i)

    _, _, x_re, x_im = lax.associative_scan(combine, (a_re, a_im, drive_re, drive_im), axis=1)
    return x_re, x_im


def s5_mixer(u_lat, u_ctx, a_re, a_im, log_step, b_re, b_im, c_re, c_im, d_skip, w_glu, b_glu, ctx_out):
    def drive_in(u):
        uf = u.astype(F32).reshape(u.shape[0], u.shape[1], S5_NG, S5_GROUP)
        return (uf, jnp.einsum('btgc,gpc->btgp', uf, b_re.astype(F32)),
                jnp.einsum('btgc,gpc->btgp', uf, b_im.astype(F32)))

    ul, bl_re, bl_im = drive_in(u_lat)
    uc, bc_re, bc_im = drive_in(u_ctx)
    zero = jnp.zeros((u_lat.shape[0], S5_NG, S5_P), F32)
    lat_states, ctx_states = [], []
    for dr in range(2):
        ab_re, ab_im, f_re, f_im = s5_discretize(a_re[dr], a_im[dr], log_step[dr])
        flip = (lambda a: a[:, ::-1]) if dr == 1 else (lambda a: a)
        cr, ci = s5_scan(flip(f_re * bc_re - f_im * bc_im), flip(f_re * bc_im + f_im * bc_re),
                         ab_re, ab_im, zero, zero)
        lr, li = s5_scan(flip(f_re * bl_re - f_im * bl_im), flip(f_re * bl_im + f_im * bl_re),
                         ab_re, ab_im, cr[:, -1], ci[:, -1])
        lat_states.append((flip(lr), flip(li)))
        if ctx_out:
            ctx_states.append((flip(cr), flip(ci)))

    def readout(uf, states):
        bsz, t = uf.shape[:2]
        x_re = states[0][0] + states[1][0]
        x_im = states[0][1] + states[1][1]
        y = (jnp.einsum('btgp,gcp->btgc', x_re, c_re.astype(F32))
             - jnp.einsum('btgp,gcp->btgc', x_im, c_im.astype(F32))
             + uf * d_skip.astype(F32).reshape(S5_NG, S5_GROUP))
        y = jax.nn.gelu(y.reshape(bsz, t, S5_CH))
        return y * jax.nn.sigmoid(y @ w_glu.astype(F32) + b_glu.astype(F32))

    out_lat = readout(ul, lat_states).astype(u_lat.dtype)
    out_ctx = readout(uc, ctx_states).astype(u_ctx.dtype) if ctx_out else None
    return out_lat, out_ctx


def attend_blocks(q, k, v):
    bsz, t = q.shape[:2]
    grp = ATT_HEADS // ATT_KV_HEADS
    nb = t // Q_BLOCK
    qb = jnp.moveaxis(q.reshape(bsz, nb, Q_BLOCK, ATT_KV_HEADS, grp, ATT_DIM), 1, 0)
    scale = ATT_DIM ** -0.5

    def one_block(qi):
        s = jnp.einsum('bqkgd,bskd->bkgqs', qi, k) * scale
        p = jax.nn.softmax(s, axis=-1)
        return jnp.einsum('bkgqs,bskd->bqkgd', p, v)

    o = lax.map(one_block, qb)
    return jnp.moveaxis(o, 0, 1).reshape(bsz, t, ATT_HEADS * ATT_DIM)


def attention_mixer(z_lat, z_ctx, q_norm_w, k_norm_w, rows, cols, ctx_out):
    def qkv(z):
        bsz, t = z.shape[:2]
        q, k, v = jnp.split(z, [ATT_HEADS * ATT_DIM, (ATT_HEADS + ATT_KV_HEADS) * ATT_DIM], axis=-1)
        q = head_rms(q.reshape(bsz, t, ATT_HEADS, ATT_DIM), q_norm_w)
        k = head_rms(k.reshape(bsz, t, ATT_KV_HEADS, ATT_DIM), k_norm_w)
        return q, k, v.astype(F32).reshape(bsz, t, ATT_KV_HEADS, ATT_DIM)

    ql, kl, vl = qkv(z_lat)
    qc, kc, vc = qkv(z_ctx)
    ql = apply_axial_rope(ql, rows, cols)
    kl = apply_axial_rope(kl, rows, cols)
    out_lat = attend_blocks(ql, jnp.concatenate([kc, kl], axis=1), jnp.concatenate([vc, vl], axis=1))
    out_ctx = attend_blocks(qc, kc, vc).astype(z_ctx.dtype) if ctx_out else None
    return out_lat.astype(z_lat.dtype), out_ctx


def depthwise_conv(h, w):
    return lax.conv_general_dilated(h, w[:, None, :].astype(h.dtype), window_strides=(1,),
                                    padding=[(CONV_W // 2, CONV_W // 2)],
                                    dimension_numbers=('NWC', 'WIO', 'NWC'),
                                    feature_group_count=h.shape[-1])


def conv_mixer(z_lat, z_ctx, conv_w, ctx_out):
    def run(z):
        b_gate, c_gate, h = jnp.split(z, 3, axis=-1)
        return b_gate * depthwise_conv(c_gate * h, conv_w)

    return run(z_lat), (run(z_ctx) if ctx_out else None)


def moe_ffn(h, w_router, b_router, w_gate_up, b_gate_up, w_down, b_down):
    n_tok, d = h.shape
    logits = (h @ w_router + b_router).astype(F32)
    top_v, top_i = lax.top_k(logits, TOP_K)
    gates = jax.nn.softmax(top_v, axis=-1)
    flat_e = top_i.reshape(-1)
    flat_tok = jnp.repeat(jnp.arange(n_tok, dtype=jnp.int32), TOP_K)
    flat_g = gates.reshape(-1)
    order = jnp.argsort(flat_e)
    sorted_e = flat_e[order]
    counts = jnp.bincount(flat_e, length=N_EXPERTS)
    padded = (counts + MOE_BLOCK - 1) // MOE_BLOCK * MOE_BLOCK
    start = jnp.cumsum(counts) - counts
    pend = jnp.cumsum(padded)
    pstart = pend - padded
    dest = pstart[sorted_e] + jnp.arange(n_tok * TOP_K, dtype=jnp.int32) - start[sorted_e]
    n_blocks = -(-(n_tok * TOP_K + N_EXPERTS * (MOE_BLOCK - 1)) // MOE_BLOCK)
    n_slots = n_blocks * MOE_BLOCK
    slot_tok = jnp.full((n_slots,), n_tok, jnp.int32).at[dest].set(flat_tok[order])
    slot_g = jnp.zeros((n_slots,), F32).at[dest].set(flat_g[order])
    block_e = jnp.minimum(jnp.searchsorted(pend, jnp.arange(n_blocks) * MOE_BLOCK, side='right'),
                          N_EXPERTS - 1)
    h_pad = jnp.concatenate([h, jnp.zeros((1, d), h.dtype)], axis=0)
    xb = h_pad[slot_tok].reshape(n_blocks, MOE_BLOCK, d)

    def expert_block(args):
        xi, e = args
        gu = xi @ w_gate_up[e] + b_gate_up[e]
        gate = jnp.minimum(gu[:, 0::2], SWIGLU_LIMIT)
        up = jnp.clip(gu[:, 1::2], -SWIGLU_LIMIT, SWIGLU_LIMIT)
        act = (up + 1.0) * gate * jax.nn.sigmoid(gate * SWIGLU_ALPHA)
        return act @ w_down[e] + b_down[e]

    yb = lax.map(expert_block, (xb, block_e)).reshape(n_slots, d)
    y = jnp.zeros((n_tok + 1, d), F32).at[slot_tok].add(yb.astype(F32) * slot_g[:, None])
    return y[:n_tok].astype(h.dtype)


def setup_inputs(seed: int = 0) -> dict:
    key = jax.random.key(seed)
    ks = jax.random.split(key, 28)
    nrm = lambda k, shape, s: jax.random.normal(k, shape, F32) * s
    D, L = D_MODEL, DEPTH
    return {
        'x': nrm(ks[0], (BATCH, SEQ, D), 1.0),
        'c': nrm(ks[1], (BATCH, D), 1.0),
        'ctx': nrm(ks[2], (BATCH, CTX_LEN, D), 1.0),
        'c_ctx': nrm(ks[3], (D,), 1.0),
        'w_mod': nrm(ks[4], (L, D, 6 * D), 0.5 * D ** -0.5),
        'b_mod': nrm(ks[5], (L, 6 * D), 0.01),
        'w_in': nrm(ks[6], (L, D, IN_COLS), D ** -0.5),
        'w_out': nrm(ks[7], (L, MIX_W, D), MIX_W ** -0.5),
        's5_a_re': -0.5 + nrm(ks[8], (L, 2, S5_NG, S5_P), 0.01),
        's5_a_im': math.pi * jnp.arange(S5_P, dtype=F32) + nrm(ks[9], (L, 2, S5_NG, S5_P), 0.01),
        's5_log_step': jax.random.uniform(ks[10], (L, 2, S5_NG), F32, math.log(1e-3), math.log(1e-1)),
        's5_b_re': nrm(ks[11], (L, S5_NG, S5_P, S5_GROUP), (2 * S5_GROUP) ** -0.5),
        's5_b_im': nrm(ks[12], (L, S5_NG, S5_P, S5_GROUP), (2 * S5_GROUP) ** -0.5),
        's5_c_re': nrm(ks[13], (L, S5_NG, S5_GROUP, S5_P), (2 * S5_P) ** -0.5),
        's5_c_im': nrm(ks[14], (L, S5_NG, S5_GROUP, S5_P), (2 * S5_P) ** -0.5),
        's5_d': nrm(ks[15], (L, S5_CH), 1.0),
        's5_w_glu': nrm(ks[16], (L, S5_CH, S5_CH), S5_CH ** -0.5),
        's5_b_glu': nrm(ks[17], (L, S5_CH), 0.01),
        'q_norm_w': 1.0 + nrm(ks[18], (L, ATT_DIM), 0.01),
        'k_norm_w': 1.0 + nrm(ks[19], (L, ATT_DIM), 0.01),
        'conv_w': nrm(ks[20], (L, CONV_W, CONV_CH), CONV_W ** -0.5),
        'w_router': nrm(ks[21], (L, D, N_EXPERTS), D ** -0.5),
        'b_router': nrm(ks[22], (L, N_EXPERTS), 0.01),
        'w_gate_up': nrm(ks[23], (L, N_EXPERTS, D, 2 * D_EXPERT), D ** -0.5),
        'b_gate_up': nrm(ks[24], (L, N_EXPERTS, 2 * D_EXPERT), 0.01),
        'w_down': nrm(ks[25], (L, N_EXPERTS, D_EXPERT, D), D_EXPERT ** -0.5),
        'b_down': nrm(ks[26], (L, N_EXPERTS, D), 0.01),
    }


def reference(x, c, ctx, c_ctx, w_mod, b_mod, w_in, w_out, s5_a_re, s5_a_im, s5_log_step,
              s5_b_re, s5_b_im, s5_c_re, s5_c_im, s5_d, s5_w_glu, s5_b_glu, q_norm_w, k_norm_w,
              conv_w, w_router, b_router, w_gate_up, b_gate_up, w_down, b_down):
    bsz, seq_len, d = x.shape
    ctx_len = ctx.shape[1]
    n_rows = seq_len // GRID_W
    pos = jnp.arange(seq_len, dtype=jnp.int32)
    rows = jnp.repeat(jnp.arange(n_rows, dtype=jnp.int32), GRID_W, total_repeat_length=seq_len)
    cols = pos % GRID_W
    act_c = jax.nn.silu(c)[:, None, :]
    act_cc = jax.nn.silu(c_ctx)[None, None, :]
    splits = [RET_COLS, RET_COLS + S5_COLS, RET_COLS + S5_COLS + ATT_COLS]
    x_lat, x_ctx = x, ctx
    for l in range(DEPTH):
        ctx_out = l < DEPTH - 1
        sh1, sc1, g1, sh2, sc2, g2 = jnp.split(act_c @ w_mod[l] + b_mod[l], 6, axis=-1)
        csh1, csc1, cg1, csh2, csc2, cg2 = jnp.split(act_cc @ w_mod[l] + b_mod[l], 6, axis=-1)
        z_lat = modulate(x_lat, sh1, sc1) @ w_in[l]
        z_ctx = modulate(x_ctx, csh1, csc1) @ w_in[l]
        zr_l, zs_l, za_l, zc_l = jnp.split(z_lat, splits, axis=-1)
        zr_c, zs_c, za_c, zc_c = jnp.split(z_ctx, splits, axis=-1)
        r_l, r_c = retention_mixer(zr_l, zr_c, pos, ctx_out)
        s_l, s_c = s5_mixer(zs_l, zs_c, s5_a_re[l], s5_a_im[l], s5_log_step[l], s5_b_re[l], s5_b_im[l],
                            s5_c_re[l], s5_c_im[l], s5_d[l], s5_w_glu[l], s5_b_glu[l], ctx_out)
        a_l, a_c = attention_mixer(za_l, za_c, q_norm_w[l], k_norm_w[l], rows, cols, ctx_out)
        v_l, v_c = conv_mixer(zc_l, zc_c, conv_w[l], ctx_out)
        x_lat = x_lat + g1 * (jnp.concatenate([r_l, s_l, a_l, v_l], axis=-1) @ w_out[l])
        f_lat = modulate(x_lat, sh2, sc2)
        if ctx_out:
            x_ctx = x_ctx + cg1 * (jnp.concatenate([r_c, s_c, a_c, v_c], axis=-1) @ w_out[l])
            f_ctx = modulate(x_ctx, csh2, csc2)
            tok = jnp.concatenate([f_ctx, f_lat], axis=1).reshape(-1, d)
            out = moe_ffn(tok, w_router[l], b_router[l], w_gate_up[l], b_gate_up[l],
                          w_down[l], b_down[l]).reshape(bsz, ctx_len + seq_len, d)
            x_ctx = x_ctx + cg2 * out[:, :ctx_len]
            x_lat = x_lat + g2 * out[:, ctx_len:]
        else:
            out = moe_ffn(f_lat.reshape(-1, d), w_router[l], b_router[l], w_gate_up[l], b_gate_up[l],
                          w_down[l], b_down[l]).reshape(bsz, seq_len, d)
            x_lat = x_lat + g2 * out
    return x_lat
```

```python
import functools
import math

import jax
import jax.numpy as jnp
from jax import lax
from jax.experimental import pallas as pl
from jax.experimental.pallas import tpu as pltpu

F32 = jnp.float32
BF16 = jnp.bfloat16

D_MODEL = 2048
DEPTH = 2
GRID_W = 64
CTX_LEN = 256
EPS = 1e-6
GROUP_W = D_MODEL // 4
RET_HEADS = 4
RET_DIM = GROUP_W // RET_HEADS
RET_CHUNK = 128
S5_GROUP = 16
S5_NG = GROUP_W // S5_GROUP
S5_P = 64
S5_STATE = S5_NG * S5_P
ATT_HEADS = 4
ATT_KV_HEADS = 2
ATT_DIM = GROUP_W // ATT_HEADS
ROPE_THETA = 10000.0
CONV_W = 3
N_EXPERTS = 32
TOP_K = 4
D_EXPERT = D_MODEL // 2
SWIGLU_LIMIT = 7.0
SWIGLU_ALPHA = 1.702
MOE_BLOCK = 128

RET_OFF = 0
S5_OFF = 5 * GROUP_W
ATT_OFF = S5_OFF + GROUP_W
ATT_K_OFF = ATT_OFF + ATT_HEADS * ATT_DIM
ATT_V_OFF = ATT_K_OFF + ATT_KV_HEADS * ATT_DIM
CONV_OFF = ATT_V_OFF + ATT_KV_HEADS * ATT_DIM
IN_COLS = CONV_OFF + 3 * GROUP_W

LANES = 128
SUBLANES = 8
MIB = 1024 * 1024
NEG_BIG = -1e30


def _cparams(semantics, vmem_mib=48):
    return pltpu.CompilerParams(dimension_semantics=semantics,
                                vmem_limit_bytes=vmem_mib * MIB)


def _pick(n, prefs):
    for p in prefs:
        if n % p == 0:
            return p
    raise ValueError(f"no tile in {prefs} divides {n}")


def _dot(a, b):
    return jnp.dot(a, b, preferred_element_type=F32)


def _dot_nt(a, b):
    return lax.dot_general(a, b, (((1,), (1,)), ((), ())), preferred_element_type=F32)


def _dot_tn(a, b):
    return lax.dot_general(a, b, (((0,), (0,)), ((), ())), preferred_element_type=F32)


def _silu(x):
    return x * jax.nn.sigmoid(x)


def _mod_body(cc_ref, w_ref, b_ref, o_ref):
    a = _silu(cc_ref[...])
    o_ref[0] = _dot(a, w_ref[0]) + b_ref[0]


def _mod_call(cc, w_mod, b_mod):
    depth, d, n = w_mod.shape
    tn = 1536
    return pl.pallas_call(
        _mod_body,
        grid=(depth, n // tn),
        in_specs=[pl.BlockSpec((SUBLANES, d), lambda l, j: (0, 0)),
                  pl.BlockSpec((1, d, tn), lambda l, j: (l, 0, j)),
                  pl.BlockSpec((1, 1, tn), lambda l, j: (l, 0, j))],
        out_specs=pl.BlockSpec((1, SUBLANES, tn), lambda l, j: (l, 0, j)),
        out_shape=jax.ShapeDtypeStruct((depth, SUBLANES, n), F32),
        compiler_params=_cparams(("arbitrary", "arbitrary")),
        name="mod_vectors",
    )(cc, w_mod, b_mod.reshape(depth, 1, n))


def _row_mod(mod_ref, row0, rows, k):
    is_ctx = (row0 + lax.broadcasted_iota(jnp.int32, (rows, 1), 0)) < CTX_LEN
    return jnp.where(is_ctx, mod_ref[1, k:k + 1, :], mod_ref[0, k:k + 1, :])


def _inproj_body(x_ref, mod_ref, w_ref, o_ref, xn_ref, *, tm):
    i = pl.program_id(0)

    @pl.when(pl.program_id(1) == 0)
    def _():
        x = x_ref[...]
        xn = x * lax.rsqrt(jnp.mean(x * x, axis=-1, keepdims=True) + EPS)
        sh = _row_mod(mod_ref, i * tm, tm, 0)
        sc = _row_mod(mod_ref, i * tm, tm, 1)
        xn_ref[...] = (xn * (1.0 + sc) + sh).astype(BF16)

    o_ref[...] = _dot(xn_ref[...], w_ref[...]).astype(BF16)


def _inproj_call(xx, mod, w_in):
    t, d = xx.shape
    n = w_in.shape[1]
    tm = _pick(t, (768, 512, 256))
    tn = 512
    return pl.pallas_call(
        functools.partial(_inproj_body, tm=tm),
        grid=(t // tm, n // tn),
        in_specs=[pl.BlockSpec((tm, d), lambda i, j: (i, 0)),
                  pl.BlockSpec((2, 6, d), lambda i, j: (0, 0, 0)),
                  pl.BlockSpec((d, tn), lambda i, j: (0, j))],
        out_specs=pl.BlockSpec((tm, tn), lambda i, j: (i, j)),
        out_shape=jax.ShapeDtypeStruct((t, n), BF16),
        scratch_shapes=[pltpu.VMEM((tm, d), BF16)],
        compiler_params=_cparams(("arbitrary", "arbitrary")),
        name="inproj",
    )(xx, mod, w_in)


def _rope_angles(pos, dim):
    freqs = ROPE_THETA ** (-jnp.arange(0, dim, 2, dtype=F32) / dim)
    return pos.astype(F32)[:, None] * freqs[None, :]


def _rope_tables(seq_len):
    pos = jnp.arange(seq_len, dtype=jnp.int32)
    ang = _rope_angles(pos, RET_DIM)
    c_ret = jnp.concatenate([jnp.cos(ang)] * 2, axis=1)
    s_ret = jnp.concatenate([-jnp.sin(ang), jnp.sin(ang)], axis=1)
    rows = pos // GRID_W
    cols = pos % GRID_W
    ar = _rope_angles(rows, ATT_DIM // 2)
    ac = _rope_angles(cols, ATT_DIM // 2)
    c_att = jnp.concatenate([jnp.cos(ar)] * 2 + [jnp.cos(ac)] * 2, axis=1)
    s_att = jnp.concatenate([-jnp.sin(ar), jnp.sin(ar), -jnp.sin(ac), jnp.sin(ac)], axis=1)
    one = jnp.ones((CTX_LEN, LANES), F32)
    zero = jnp.zeros((CTX_LEN, LANES), F32)
    cat = lambda a, b: jnp.concatenate([a, b], axis=0)
    return cat(one, c_ret), cat(zero, s_ret), cat(one, c_att), cat(zero, s_att)


def _ret_tables(rc):
    lg = jnp.log(1.0 - 2.0 ** (-5.0 - jnp.arange(RET_HEADS, dtype=F32)))
    idx = jnp.arange(rc, dtype=F32)
    diff = idx[:, None] - idx[None, :]
    dec, zeta, xi, gch = [], [], [], []
    for direction, l in enumerate((lg, lg[::-1])):
        if direction == 0:
            dec.append(jnp.where(diff >= 0, jnp.exp(jnp.maximum(diff, 0.0)[None] * l[:, None, None]), 0.0))
            zeta.append(jnp.exp((rc - 1.0 - idx)[None, :] * l[:, None]))
            xi.append(jnp.exp((idx + 1.0)[None, :] * l[:, None]))
        else:
            dec.append(jnp.where(diff <= 0, jnp.exp(jnp.maximum(-diff, 0.0)[None] * l[:, None, None]), 0.0))
            zeta.append(jnp.exp(idx[None, :] * l[:, None]))
            xi.append(jnp.exp((rc - idx)[None, :] * l[:, None]))
        gch.append(jnp.exp(rc * l))
    dec = jnp.concatenate(dec, axis=0)
    bc = lambda a: jnp.broadcast_to(jnp.concatenate(a, axis=0)[:, :, None], (2 * RET_HEADS, rc, LANES))
    gch = jnp.broadcast_to(jnp.concatenate(gch, axis=0)[:, None, None], (2 * RET_HEADS, SUBLANES, LANES))
    return dec, bc(zeta), bc(xi), gch


def _rope_half(x, c, s):
    return x * c + pltpu.roll(x, RET_DIM // 2, axis=1) * s


def _group_norm(y):
    yc = y - jnp.mean(y, axis=-1, keepdims=True)
    return yc * lax.rsqrt(jnp.mean(yc * yc, axis=-1, keepdims=True) + EPS)


def _ret_body(qf, kf, vf, gf, cf, sf, qb, kb, vb, gb, cb, sb, dec_ref, zeta_ref, xi_ref, gch_ref,
              of_ref, ob_ref, state_ref):
    @pl.when(pl.program_id(0) == 0)
    def _():
        state_ref[...] = jnp.zeros_like(state_ref)

    dirs = ((qf, kf, vf, gf, cf, sf, of_ref), (qb, kb, vb, gb, cb, sb, ob_ref))
    for direction, (q_ref, k_ref, v_ref, g_ref, c_ref, s_ref, o_ref) in enumerate(dirs):
        c = c_ref[...]
        s = s_ref[...]
        for h in range(RET_HEADS):
            t = direction * RET_HEADS + h
            sl = slice(h * RET_DIM, (h + 1) * RET_DIM)
            q = _rope_half(q_ref[:, sl].astype(F32), c, s).astype(BF16)
            k = _rope_half(k_ref[:, sl].astype(F32), c, s) * RET_DIM ** -0.5
            v = v_ref[:, sl]
            state = state_ref[t]
            scores = _dot_nt(q, k.astype(BF16)) * dec_ref[t]
            y = _dot(scores.astype(BF16), v) + _dot(q, state.astype(BF16)) * xi_ref[t]
            kv = _dot_tn((k * zeta_ref[t]).astype(BF16), v)
            state_ref[t] = state * gch_ref[t][0:1, :] + kv
            g = g_ref[:, sl].astype(F32)
            o_ref[:, sl] = (_silu(g) * _group_norm(y)).astype(BF16)


def _ret_call(z, c_ret, s_ret, tables):
    t = z.shape[0]
    rc = RET_CHUNK
    nch = t // rc
    nctx = CTX_LEN // rc
    dec, zeta, xi, gch = tables

    def bwd(s):
        return jnp.where(s < nctx, nctx - 1 - s, nch + nctx - 1 - s)

    zf = lambda col: pl.BlockSpec((rc, GROUP_W), lambda s, col=col: (s, col))
    zb = lambda col: pl.BlockSpec((rc, GROUP_W), lambda s, col=col: (bwd(s), col))
    tf = pl.BlockSpec((rc, LANES), lambda s: (s, 0))
    tb = pl.BlockSpec((rc, LANES), lambda s: (bwd(s), 0))
    full = lambda a: pl.BlockSpec(a.shape, lambda s: (0,) * a.ndim)
    out = jax.ShapeDtypeStruct((t, GROUP_W), BF16)
    return pl.pallas_call(
        _ret_body,
        grid=(nch,),
        in_specs=[zf(0), zf(1), zf(2), zf(3), tf, tf, zb(0), zb(1), zb(2), zb(4), tb, tb,
                  full(dec), full(zeta), full(xi), full(gch)],
        out_specs=[pl.BlockSpec((rc, GROUP_W), lambda s: (s, 0)),
                   pl.BlockSpec((rc, GROUP_W), lambda s: (bwd(s), 0))],
        out_shape=[out, out],
        scratch_shapes=[pltpu.VMEM((2 * RET_HEADS, RET_DIM, RET_DIM), F32)],
        compiler_params=_cparams(("arbitrary",)),
        name="retention",
    )(z, z, z, z, c_ret, s_ret, z, z, z, z, c_ret, s_ret, dec, zeta, xi, gch)


S5_BLK = 4
S5_BLK_IN = GROUP_W // S5_BLK
S5_BLK_ST = S5_STATE // S5_BLK
S5_ROWS = SUBLANES
S5_NCOEF = 8


def _block_diag(w, rows_per_group, cols_per_group):
    gpb = S5_NG // S5_BLK
    w4 = w.reshape(S5_BLK, gpb, rows_per_group, cols_per_group)
    eye = jnp.eye(gpb, dtype=w.dtype)
    out = w4[:, :, :, None, :] * eye[None, :, None, :, None]
    return out.reshape(S5_BLK, gpb * rows_per_group, gpb * cols_per_group)


def _s5_prep_body(are_ref, aim_ref, ls_ref, wbr_ref, wbi_ref, coef_ref, wfr_ref, wfi_ref):
    a_re = are_ref[...]
    a_im = aim_ref[...]
    step = jnp.exp(ls_ref[...])
    mag = jnp.exp(a_re * step)
    ab_re = mag * jnp.cos(a_im * step)
    ab_im = mag * jnp.sin(a_im * step)
    den = a_re * a_re + a_im * a_im
    num_re = ab_re - 1.0
    f_re = (num_re * a_re + ab_im * a_im) / den
    f_im = (ab_im * a_re - num_re * a_im) / den
    row = lax.broadcasted_iota(jnp.int32, (S5_ROWS, S5_STATE), 0)
    for d in range(2):
        pr = [ab_re[d:d + 1]]
        pi = [ab_im[d:d + 1]]
        for _ in range(S5_ROWS - 1):
            pr.append(pr[-1] * ab_re[d:d + 1] - pi[-1] * ab_im[d:d + 1])
            pi.append(pr[-2] * ab_im[d:d + 1] + pi[-1] * ab_re[d:d + 1])
        zero = jnp.zeros((S5_ROWS, S5_STATE), F32)
        for n, shift in enumerate((1, 2, 4)):
            keep = (row >= shift) if d == 0 else (row < S5_ROWS - shift)
            coef_ref[d, 2 * n] = jnp.where(keep, pr[shift - 1], zero)
            coef_ref[d, 2 * n + 1] = jnp.where(keep, pi[shift - 1], zero)
        cr, ci = zero, zero
        for r in range(S5_ROWS):
            p = r if d == 0 else S5_ROWS - 1 - r
            cr = jnp.where(row == r, pr[p], cr)
            ci = jnp.where(row == r, pi[p], ci)
        coef_ref[d, 6] = cr
        coef_ref[d, 7] = ci
        for b in range(S5_BLK):
            cs = slice(b * S5_BLK_ST, (b + 1) * S5_BLK_ST)
            fr = f_re[d:d + 1, cs]
            fi = f_im[d:d + 1, cs]
            wfr_ref[d, b] = (wbr_ref[b] * fr - wbi_ref[b] * fi).astype(BF16)
            wfi_ref[d, b] = (wbr_ref[b] * fi + wbi_ref[b] * fr).astype(BF16)


def _s5_prep_call(a_re, a_im, log_step, b_re, b_im):
    flat = lambda a: a.astype(F32).reshape(2, S5_STATE)
    ls = jnp.repeat(log_step.astype(F32), S5_P, axis=-1)
    wbr = _block_diag(jnp.swapaxes(b_re.astype(F32), 1, 2), S5_GROUP, S5_P)
    wbi = _block_diag(jnp.swapaxes(b_im.astype(F32), 1, 2), S5_GROUP, S5_P)
    wshape = jax.ShapeDtypeStruct((2, S5_BLK, S5_BLK_IN, S5_BLK_ST), BF16)
    return pl.pallas_call(
        _s5_prep_body,
        out_shape=[jax.ShapeDtypeStruct((2, S5_NCOEF, S5_ROWS, S5_STATE), F32), wshape, wshape],
        compiler_params=pltpu.CompilerParams(vmem_limit_bytes=32 * MIB),
        name="s5_prep",
    )(flat(a_re), flat(a_im), ls, wbr, wbi)


def _s5_scan_body(uf_ref, ub_ref, coef_ref, wfr_ref, wfi_ref, wcr_ref, wci_ref, yf_ref, yb_ref,
                  dr_ref, di_ref, car_ref, *, tm):
    @pl.when(pl.program_id(0) == 0)
    def _():
        car_ref[...] = jnp.zeros_like(car_ref)

    nblk = tm // S5_ROWS
    for d, (u_ref, y_ref) in enumerate(((uf_ref, yf_ref), (ub_ref, yb_ref))):
        for b in range(S5_BLK):
            ub = u_ref[:, b * S5_BLK_IN:(b + 1) * S5_BLK_IN]
            cs = slice(b * S5_BLK_ST, (b + 1) * S5_BLK_ST)
            dr_ref[d, :, cs] = _dot(ub, wfr_ref[d, b])
            di_ref[d, :, cs] = _dot(ub, wfi_ref[d, b])

        def block(n, carry, d=d):
            c_re, c_im = carry
            blk = n if d == 0 else nblk - 1 - n
            r0 = pl.multiple_of(blk * S5_ROWS, S5_ROWS)
            xr = dr_ref[d, pl.ds(r0, S5_ROWS), :]
            xi = di_ref[d, pl.ds(r0, S5_ROWS), :]
            for n_s, shift in enumerate((1, 2, 4)):
                ar = coef_ref[d, 2 * n_s]
                ai = coef_ref[d, 2 * n_s + 1]
                amt = shift if d == 0 else S5_ROWS - shift
                sr = pltpu.roll(xr, amt, axis=0)
                si = pltpu.roll(xi, amt, axis=0)
                xr, xi = xr + ar * sr - ai * si, xi + ar * si + ai * sr
            pr = coef_ref[d, 6]
            pi = coef_ref[d, 7]
            xr, xi = xr + pr * c_re - pi * c_im, xi + pr * c_im + pi * c_re
            dr_ref[d, pl.ds(r0, S5_ROWS), :] = xr
            di_ref[d, pl.ds(r0, S5_ROWS), :] = xi
            last = S5_ROWS - 1 if d == 0 else 0
            return xr[last:last + 1, :], xi[last:last + 1, :]

        c_re, c_im = lax.fori_loop(0, nblk, block, (car_ref[d, 0:1, :], car_ref[d, 1:2, :]))
        car_ref[d, 0:1, :] = c_re
        car_ref[d, 1:2, :] = c_im
        for b in range(S5_BLK):
            cs = slice(b * S5_BLK_ST, (b + 1) * S5_BLK_ST)
            y_ref[:, b * S5_BLK_IN:(b + 1) * S5_BLK_IN] = (
                _dot(dr_ref[d, :, cs].astype(BF16), wcr_ref[b]) - _dot(di_ref[d, :, cs].astype(BF16), wci_ref[b]))


def _s5_scan_call(z, coef, wfr, wfi, wcr, wci):
    t = z.shape[0]
    tm = CTX_LEN
    nt = t // tm
    nctx = CTX_LEN // tm
    ucol = S5_OFF // GROUP_W

    def bwd(s):
        return jnp.where(s < nctx, nctx - 1 - s, nt + nctx - 1 - s)

    full = lambda a: pl.BlockSpec(a.shape, lambda s: (0,) * a.ndim)
    out = jax.ShapeDtypeStruct((t, GROUP_W), F32)
    return pl.pallas_call(
        functools.partial(_s5_scan_body, tm=tm),
        grid=(nt,),
        in_specs=[pl.BlockSpec((tm, GROUP_W), lambda s: (s, ucol)),
                  pl.BlockSpec((tm, GROUP_W), lambda s: (bwd(s), ucol)),
                  full(coef), full(wfr), full(wfi), full(wcr), full(wci)],
        out_specs=[pl.BlockSpec((tm, GROUP_W), lambda s: (s, 0)),
                   pl.BlockSpec((tm, GROUP_W), lambda s: (bwd(s), 0))],
        out_shape=[out, out],
        scratch_shapes=[pltpu.VMEM((2, tm, S5_STATE), F32), pltpu.VMEM((2, tm, S5_STATE), F32),
                        pltpu.VMEM((2, SUBLANES, S5_STATE), F32)],
        compiler_params=_cparams(("arbitrary",)),
        name="s5_scan",
    )(z, z, coef, wfr, wfi, wcr, wci)


def _gelu_tanh(x):
    return 0.5 * x * (1.0 + jnp.tanh(math.sqrt(2.0 / math.pi) * (x + 0.044715 * (x * x * x))))


def _s5_finish_body(yf_ref, yb_ref, u_ref, d_ref, w_ref, b_ref, o_ref):
    y = yf_ref[...] + yb_ref[...] + u_ref[...].astype(F32) * d_ref[...]
    y = _gelu_tanh(y)
    o_ref[...] = (y * jax.nn.sigmoid(_dot(y.astype(BF16), w_ref[...]) + b_ref[...])).astype(BF16)


def _s5_finish_call(yf, yb, z, d_skip, w_glu, b_glu):
    t = z.shape[0]
    tm = _pick(t, (768, 512, 256))
    ucol = S5_OFF // GROUP_W
    row = pl.BlockSpec((tm, GROUP_W), lambda i: (i, 0))
    vec = pl.BlockSpec((1, GROUP_W), lambda i: (0, 0))
    return pl.pallas_call(
        _s5_finish_body,
        grid=(t // tm,),
        in_specs=[row, row, pl.BlockSpec((tm, GROUP_W), lambda i: (i, ucol)), vec,
                  pl.BlockSpec((GROUP_W, GROUP_W), lambda i: (0, 0)), vec],
        out_specs=row,
        out_shape=jax.ShapeDtypeStruct((t, GROUP_W), BF16),
        compiler_params=_cparams(("arbitrary",)),
        name="s5_finish",
    )(yf, yb, z, d_skip.astype(F32).reshape(1, GROUP_W), w_glu.astype(BF16),
      b_glu.astype(F32).reshape(1, GROUP_W))


def _head_rms_rope(x, w, c, s, lane):
    xn = x * lax.rsqrt(jnp.mean(x * x, axis=-1, keepdims=True) + EPS) * w
    quarter = ATT_DIM // 4
    first = (lane % (2 * quarter)) < quarter
    partner = jnp.where(first, pltpu.roll(xn, ATT_DIM - quarter, axis=1), pltpu.roll(xn, quarter, axis=1))
    return xn * c + partner * s


def _att_prep_body(q_ref, k_ref, c_ref, s_ref, qw_ref, kw_ref, qo_ref, ko_ref):
    c = c_ref[...]
    s = s_ref[...]
    lane = lax.broadcasted_iota(jnp.int32, c.shape, 1)
    for h in range(ATT_HEADS):
        sl = slice(h * ATT_DIM, (h + 1) * ATT_DIM)
        q = _head_rms_rope(q_ref[:, sl].astype(F32), qw_ref[...], c, s, lane)
        qo_ref[:, sl] = (q * ATT_DIM ** -0.5).astype(BF16)
    for h in range(ATT_KV_HEADS):
        sl = slice(h * ATT_DIM, (h + 1) * ATT_DIM)
        ko_ref[:, sl] = _head_rms_rope(k_ref[:, sl].astype(F32), kw_ref[...], c, s, lane).astype(BF16)


def _att_prep_call(z, c_att, s_att, q_norm_w, k_norm_w):
    t = z.shape[0]
    tm = _pick(t, (768, 512, 256))
    qw, kw = ATT_HEADS * ATT_DIM, ATT_KV_HEADS * ATT_DIM
    tab = pl.BlockSpec((tm, LANES), lambda i: (i, 0))
    vec = pl.BlockSpec((1, ATT_DIM), lambda i: (0, 0))
    return pl.pallas_call(
        _att_prep_body,
        grid=(t // tm,),
        in_specs=[pl.BlockSpec((tm, qw), lambda i: (i, ATT_OFF // qw)),
                  pl.BlockSpec((tm, kw), lambda i: (i, ATT_K_OFF // kw)), tab, tab, vec, vec],
        out_specs=[pl.BlockSpec((tm, qw), lambda i: (i, 0)), pl.BlockSpec((tm, kw), lambda i: (i, 0))],
        out_shape=[jax.ShapeDtypeStruct((t, qw), BF16), jax.ShapeDtypeStruct((t, kw), BF16)],
        compiler_params=_cparams(("arbitrary",)),
        name="att_prep",
    )(z, z, c_att, s_att, q_norm_w.astype(F32).reshape(1, ATT_DIM), k_norm_w.astype(F32).reshape(1, ATT_DIM))


def _flash_body(q_ref, k_ref, v_ref, o_ref, *, tq, tk, n_lat):
    grp = ATT_HEADS // ATT_KV_HEADS
    q2 = jnp.concatenate([q_ref[:, g * ATT_DIM:(g + 1) * ATT_DIM] for g in range(grp)], axis=0)

    def attend(chunk, nchunks):
        def step(n, carry):
            m, l, acc = carry
            off = pl.multiple_of(n * chunk, chunk)
            k = k_ref[pl.ds(off, chunk), :]
            v = v_ref[pl.ds(off, chunk), :]
            s = _dot_nt(q2, k)
            mn = jnp.maximum(m, jnp.max(s, axis=1, keepdims=True))
            alpha = jnp.exp(m - mn)
            p = jnp.exp(s - mn)
            l = alpha * l + jnp.sum(p, axis=1, keepdims=True)
            acc = alpha * acc + _dot(p.astype(BF16), v)
            return mn, l, acc

        init = (jnp.full((grp * tq, 1), NEG_BIG, F32), jnp.zeros((grp * tq, 1), F32),
                jnp.zeros((grp * tq, ATT_DIM), F32))
        _, l, acc = lax.fori_loop(0, nchunks, step, init)
        o = acc / l
        for g in range(grp):
            o_ref[:, g * ATT_DIM:(g + 1) * ATT_DIM] = o[g * tq:(g + 1) * tq].astype(BF16)

    is_ctx = pl.program_id(1) < CTX_LEN // tq

    @pl.when(is_ctx)
    def _():
        attend(CTX_LEN, 1)

    @pl.when(jnp.logical_not(is_ctx))
    def _():
        attend(tk, n_lat)


def _flash_call(qn, kn, z):
    t = z.shape[0]
    tq = CTX_LEN
    tk = _pick(t, (768, 512, 256))
    gw = (ATT_HEADS // ATT_KV_HEADS) * ATT_DIM
    vcol = ATT_V_OFF // ATT_DIM
    return pl.pallas_call(
        functools.partial(_flash_body, tq=tq, tk=tk, n_lat=t // tk),
        grid=(ATT_KV_HEADS, t // tq),
        in_specs=[pl.BlockSpec((tq, gw), lambda h, i: (i, h)),
                  pl.BlockSpec((t, ATT_DIM), lambda h, i: (0, h)),
                  pl.BlockSpec((t, ATT_DIM), lambda h, i: (0, vcol + h))],
        out_specs=pl.BlockSpec((tq, gw), lambda h, i: (i, h)),
        out_shape=jax.ShapeDtypeStruct((t, ATT_HEADS * ATT_DIM), BF16),
        compiler_params=_cparams(("arbitrary", "arbitrary")),
        name="flash_attention",
    )(qn, kn, z)


def _conv_body(b_ref, c_ref, h_ref, w_ref, o_ref):
    t = b_ref.shape[0]
    p = c_ref[...].astype(F32) * h_ref[...].astype(F32)
    row = lax.broadcasted_iota(jnp.int32, p.shape, 0)
    seq_first = (row == 0) | (row == CTX_LEN)
    seq_last = (row == CTX_LEN - 1) | (row == t - 1)
    prev = jnp.where(seq_first, 0.0, pltpu.roll(p, 1, axis=0))
    nxt = jnp.where(seq_last, 0.0, pltpu.roll(p, t - 1, axis=0))
    w = w_ref[...]
    y = w[0:1, :] * prev + w[1:2, :] * p + w[2:3, :] * nxt
    o_ref[...] = (b_ref[...].astype(F32) * y).astype(BF16)


def _conv_call(z, conv_w):
    t = z.shape[0]
    col0 = CONV_OFF // LANES
    nblk = GROUP_W // LANES
    zc = lambda k: pl.BlockSpec((t, LANES), lambda j, k=k: (0, col0 + k * nblk + j))
    return pl.pallas_call(
        _conv_body,
        grid=(nblk,),
        in_specs=[zc(0), zc(1), zc(2), pl.BlockSpec((CONV_W, LANES), lambda j: (0, j))],
        out_specs=pl.BlockSpec((t, LANES), lambda j: (0, j)),
        out_shape=jax.ShapeDtypeStruct((t, GROUP_W), BF16),
        compiler_params=_cparams(("arbitrary",)),
        name="gated_conv",
    )(z, z, z, conv_w.astype(F32))


ROUTE_IDX, ROUTE_GATE, ROUTE_RANK = 0, TOP_K, 2 * TOP_K


def _outproj_body(rf_ref, rb_ref, s_ref, a_ref, v_ref, x_ref, mod_ref, w_ref, wr_ref, br_ref, tri_ref,
                  xo_ref, fo_ref, ro_ref, cnt_ref, count_ref, *, tm):
    i = pl.program_id(0)

    @pl.when(i == 0)
    def _():
        count_ref[...] = jnp.zeros_like(count_ref)

    r = (rf_ref[...].astype(F32) + rb_ref[...].astype(F32)).astype(BF16)
    g = GROUP_W
    mix = (_dot(r, w_ref[0:g, :]) + _dot(s_ref[...], w_ref[g:2 * g, :])
           + _dot(a_ref[...], w_ref[2 * g:3 * g, :]) + _dot(v_ref[...], w_ref[3 * g:4 * g, :]))
    xn = x_ref[...] + _row_mod(mod_ref, i * tm, tm, 2) * mix
    xo_ref[...] = xn
    f = xn * lax.rsqrt(jnp.mean(xn * xn, axis=-1, keepdims=True) + EPS)
    f = f * (1.0 + _row_mod(mod_ref, i * tm, tm, 4)) + _row_mod(mod_ref, i * tm, tm, 3)
    fo_ref[...] = f

    fh = f.astype(BF16)
    fl = (f - fh.astype(F32)).astype(BF16)
    d = f.shape[1]
    logits = (_dot(fh, wr_ref[0:d, :]) + _dot(fl, wr_ref[0:d, :]) + _dot(fh, wr_ref[d:2 * d, :])
              + br_ref[...])
    lane = lax.broadcasted_iota(jnp.int32, logits.shape, 1).astype(F32)
    vals = logits
    top_v, top_i, hits = [], [], []
    for _ in range(TOP_K):
        mk = jnp.max(vals, axis=1, keepdims=True)
        ik = jnp.min(jnp.where(vals == mk, lane, float(LANES)), axis=1, keepdims=True)
        hit = lane == ik
        top_v.append(mk)
        top_i.append(ik)
        hits.append(hit)
        vals = jnp.where(hit, 2.0 * NEG_BIG, vals)
    ex = [jnp.exp(v - top_v[0]) for v in top_v]
    den = ex[0] + ex[1] + ex[2] + ex[3]
    member = jnp.zeros_like(logits)
    for hit in hits:
        member = jnp.where(hit, 1.0, member)
    ahead = _dot(tri_ref[...], member.astype(BF16)) + count_ref[0:1, :]
    record = jnp.zeros_like(logits)
    for k in range(TOP_K):
        rank = jnp.sum(jnp.where(hits[k], ahead, 0.0), axis=1, keepdims=True)
        record = jnp.where(lane == float(ROUTE_IDX + k), top_i[k], record)
        record = jnp.where(lane == float(ROUTE_GATE + k), ex[k] / den, record)
        record = jnp.where(lane == float(ROUTE_RANK + k), rank, record)
    ro_ref[...] = record
    count_ref[...] = count_ref[...] + jnp.sum(member, axis=0, keepdims=True)
    cnt_ref[...] = count_ref[...]


def _outproj_call(rf, rb, s5, att, cv, xx, mod, w_out, w_router, b_router):
    t, d = xx.shape
    tm = CTX_LEN
    wr = jnp.zeros((d, LANES), F32).at[:, :N_EXPERTS].set(w_router.astype(F32))
    wr_hi = wr.astype(BF16)
    wr_lo = (wr - wr_hi.astype(F32)).astype(BF16)
    wr2 = jnp.concatenate([wr_hi, wr_lo], axis=0)
    br = jnp.full((1, LANES), NEG_BIG, F32).at[0, :N_EXPERTS].set(b_router.astype(F32))
    tri = jnp.tri(tm, k=-1, dtype=BF16)
    mix = pl.BlockSpec((tm, GROUP_W), lambda i: (i, 0))
    row = pl.BlockSpec((tm, d), lambda i: (i, 0))
    full = lambda a: pl.BlockSpec(a.shape, lambda i: (0,) * a.ndim)
    rec = pl.BlockSpec((tm, LANES), lambda i: (i, 0))
    return pl.pallas_call(
        functools.partial(_outproj_body, tm=tm),
        grid=(t // tm,),
        in_specs=[mix, mix, mix, mix, mix, row, full(mod), full(w_out), full(wr2), full(br), full(tri)],
        out_specs=[row, row, rec, pl.BlockSpec((SUBLANES, LANES), lambda i: (0, 0))],
        out_shape=[jax.ShapeDtypeStruct((t, d), F32), jax.ShapeDtypeStruct((t, d), F32),
                   jax.ShapeDtypeStruct((t, LANES), F32), jax.ShapeDtypeStruct((SUBLANES, LANES), F32)],
        scratch_shapes=[pltpu.VMEM((SUBLANES, LANES), F32)],
        compiler_params=_cparams(("arbitrary",)),
        name="outproj_route",
    )(rf, rb, s5, att, cv, xx, mod, w_out, wr2, br, tri)


def _row_copy(src_hbm, row, dst, dst_row, sem):
    return pltpu.make_async_copy(src_hbm.at[pl.ds(row, 1), :], dst.at[pl.ds(dst_row, 1), :], sem)


def _moe_body(tok_ref, be_ref, nu_ref, f_hbm, wg_ref, wu_ref, wd_ref, bg_ref, bu_ref, bd_ref, o_ref,
              xbuf, sem):
    b = pl.program_id(0)
    n_used = nu_ref[0]

    def issue(blk, slot):
        def one(r, carry):
            _row_copy(f_hbm, tok_ref[blk * MOE_BLOCK + r], xbuf.at[slot], r, sem.at[slot]).start()
            return carry
        lax.fori_loop(0, MOE_BLOCK, one, 0)

    def wait(slot):
        def one(r, carry):
            _row_copy(f_hbm, 0, xbuf.at[slot], r, sem.at[slot]).wait()
            return carry
        lax.fori_loop(0, MOE_BLOCK, one, 0)

    slot = b % 2

    @pl.when((b == 0) & (n_used > 0))
    def _():
        issue(0, 0)

    @pl.when(b < n_used)
    def _():
        wait(slot)

        @pl.when(b + 1 < n_used)
        def _():
            issue(b + 1, 1 - slot)

        x = xbuf[slot].astype(BF16)
        gate = jnp.minimum(_dot(x, wg_ref[0]) + bg_ref[0], SWIGLU_LIMIT)
        up = jnp.clip(_dot(x, wu_ref[0]) + bu_ref[0], -SWIGLU_LIMIT, SWIGLU_LIMIT)
        act = (up + 1.0) * gate * jax.nn.sigmoid(gate * SWIGLU_ALPHA)
        o_ref[...] = _dot(act.astype(BF16), wd_ref[0]) + bd_ref[0]

    @pl.when(b >= n_used)
    def _():
        o_ref[...] = jnp.zeros_like(o_ref)


def _moe_call(slot_tok, block_e, n_used, f, wg, wu, wd, bg, bu, bd):
    n_blocks = block_e.shape[0]
    d = f.shape[1]
    e_spec = lambda shape: pl.BlockSpec((1,) + shape, lambda b, tok, be, nu: (be[b], 0, 0))
    grid_spec = pltpu.PrefetchScalarGridSpec(
        num_scalar_prefetch=3,
        grid=(n_blocks,),
        in_specs=[pl.BlockSpec(memory_space=pl.ANY),
                  e_spec((d, D_EXPERT)), e_spec((d, D_EXPERT)), e_spec((D_EXPERT, d)),
                  e_spec((1, D_EXPERT)), e_spec((1, D_EXPERT)), e_spec((1, d))],
        out_specs=pl.BlockSpec((MOE_BLOCK, d), lambda b, tok, be, nu: (b, 0)),
        scratch_shapes=[pltpu.VMEM((2, MOE_BLOCK, d), F32), pltpu.SemaphoreType.DMA((2,))],
    )
    return pl.pallas_call(
        _moe_body,
        grid_spec=grid_spec,
        out_shape=jax.ShapeDtypeStruct((n_blocks * MOE_BLOCK, d), F32),
        compiler_params=_cparams(("arbitrary",)),
        name="moe_experts",
    )(slot_tok, block_e, n_used, f, wg, wu, wd, bg, bu, bd)


COMBINE_TOKENS = 64


def _combine_body(dest_ref, y_hbm, gate_ref, x_ref, mod_ref, o_ref, buf, sem):
    i = pl.program_id(0)
    n = pl.num_programs(0)
    tt = COMBINE_TOKENS

    def issue(tile, slot):
        def one(r, carry):
            for k in range(TOP_K):
                _row_copy(y_hbm, dest_ref[(tile * tt + r) * TOP_K + k], buf.at[slot, k], r, sem.at[slot]).start()
            return carry
        lax.fori_loop(0, tt, one, 0)

    def wait(slot):
        def one(r, carry):
            for k in range(TOP_K):
                _row_copy(y_hbm, 0, buf.at[slot, k], r, sem.at[slot]).wait()
            return carry
        lax.fori_loop(0, tt, one, 0)

    slot = i % 2

    @pl.when(i == 0)
    def _():
        issue(0, 0)

    wait(slot)

    @pl.when(i + 1 < n)
    def _():
        issue(i + 1, 1 - slot)

    gates = gate_ref[...]
    y = buf[slot, 0] * gates[:, 0:1]
    for k in range(1, TOP_K):
        y = y + buf[slot, k] * gates[:, k:k + 1]
    o_ref[...] = x_ref[...] + mod_ref[0, 5:6, :] * y


def _combine_call(dest, yb, gates, xnew, mod):
    t, d = xnew.shape
    tt = COMBINE_TOKENS
    nctx = CTX_LEN // tt
    grid_spec = pltpu.PrefetchScalarGridSpec(
        num_scalar_prefetch=1,
        grid=(t // tt,),
        in_specs=[pl.BlockSpec(memory_space=pl.ANY),
                  pl.BlockSpec((tt, TOP_K), lambda i, dest: (i, 0)),
                  pl.BlockSpec((tt, d), lambda i, dest: (i, 0)),
                  pl.BlockSpec((1, 6, d), lambda i, dest: (jnp.where(i < nctx, 1, 0), 0, 0))],
        out_specs=pl.BlockSpec((tt, d), lambda i, dest: (i, 0)),
        scratch_shapes=[pltpu.VMEM((2, TOP_K, tt, d), F32), pltpu.SemaphoreType.DMA((2,))],
    )
    return pl.pallas_call(
        _combine_body,
        grid_spec=grid_spec,
        out_shape=jax.ShapeDtypeStruct((t, d), F32),
        compiler_params=_cparams(("arbitrary",)),
        name="moe_combine",
    )(dest, yb, gates, xnew, mod)


def _routing_tables(record, counts, t):
    n_blocks = -(-(t * TOP_K + N_EXPERTS * (MOE_BLOCK - 1)) // MOE_BLOCK)
    top_i = record[:, ROUTE_IDX:ROUTE_IDX + TOP_K].astype(jnp.int32)
    gates = record[:, ROUTE_GATE:ROUTE_GATE + TOP_K]
    rank = record[:, ROUTE_RANK:ROUTE_RANK + TOP_K].astype(jnp.int32)
    cnt = counts[0, :N_EXPERTS].astype(jnp.int32)
    padded = (cnt + MOE_BLOCK - 1) // MOE_BLOCK * MOE_BLOCK
    pend = jnp.cumsum(padded)
    pstart = pend - padded
    dest = (pstart[top_i] + rank).reshape(-1)
    flat_tok = jnp.repeat(jnp.arange(t, dtype=jnp.int32), TOP_K)
    slot_tok = jnp.zeros((n_blocks * MOE_BLOCK,), jnp.int32).at[dest].set(flat_tok)
    block_e = jnp.minimum(jnp.searchsorted(pend, jnp.arange(n_blocks) * MOE_BLOCK, side='right'),
                          N_EXPERTS - 1).astype(jnp.int32)
    n_used = (pend[-1:] // MOE_BLOCK).astype(jnp.int32)
    return dest.astype(jnp.int32), gates, slot_tok, block_e, n_used


def kernel(x, c, ctx, c_ctx, w_mod, b_mod, w_in, w_out, s5_a_re, s5_a_im, s5_log_step, s5_b_re, s5_b_im,
           s5_c_re, s5_c_im, s5_d, s5_w_glu, s5_b_glu, q_norm_w, k_norm_w, conv_w, w_router, b_router,
           w_gate_up, b_gate_up, w_down, b_down):
    bsz, seq_len, d = x.shape
    assert bsz == 1 and d == D_MODEL and ctx.shape[1] == CTX_LEN
    t = CTX_LEN + seq_len
    xx = jnp.concatenate([ctx[0], x[0]], axis=0).astype(F32)

    cc = jnp.zeros((SUBLANES, d), F32).at[0].set(c[0]).at[1].set(c_ctx)
    mods = _mod_call(cc, w_mod.astype(F32), b_mod.astype(F32))[:, :2].reshape(DEPTH, 2, 6, d)
    c_ret, s_ret, c_att, s_att = _rope_tables(seq_len)
    ret_tables = _ret_tables(RET_CHUNK)

    for l in range(DEPTH):
        mod = mods[l]
        z = _inproj_call(xx, mod, w_in[l].astype(BF16))
        rf, rb = _ret_call(z, c_ret, s_ret, ret_tables)
        coef, wfr, wfi = _s5_prep_call(s5_a_re[l], s5_a_im[l], s5_log_step[l], s5_b_re[l], s5_b_im[l])
        wcr = _block_diag(jnp.swapaxes(s5_c_re[l].astype(F32), 1, 2), S5_P, S5_GROUP).astype(BF16)
        wci = _block_diag(jnp.swapaxes(s5_c_im[l].astype(F32), 1, 2), S5_P, S5_GROUP).astype(BF16)
        yf, yb = _s5_scan_call(z, coef, wfr, wfi, wcr, wci)
        s5 = _s5_finish_call(yf, yb, z, s5_d[l], s5_w_glu[l], s5_b_glu[l])
        qn, kn = _att_prep_call(z, c_att, s_att, q_norm_w[l], k_norm_w[l])
        att = _flash_call(qn, kn, z)
        cv = _conv_call(z, conv_w[l])
        xnew, f, record, counts = _outproj_call(rf, rb, s5, att, cv, xx, mod, w_out[l].astype(BF16),
                                                w_router[l], b_router[l])
        dest, gates, slot_tok, block_e, n_used = _routing_tables(record, counts, t)
        wg = w_gate_up[l][:, :, 0::2].astype(BF16)
        wu = w_gate_up[l][:, :, 1::2].astype(BF16)
        bg = b_gate_up[l][:, None, 0::2].astype(F32)
        bu = b_gate_up[l][:, None, 1::2].astype(F32)
        y_slots = _moe_call(slot_tok, block_e, n_used, f, wg, wu, w_down[l].astype(BF16), bg, bu,
                            b_down[l][:, None, :].astype(F32))
        xx = _combine_call(dest, y_slots, gates, xnew, mod)

    return xx[CTX_LEN:][None].astype(x.dtype)
```

```python
import functools
import math

import jax
import jax.numpy as jnp
from jax import lax
from jax.experimental import pallas as pl
from jax.experimental.pallas import tpu as pltpu

F32 = jnp.float32
BF16 = jnp.bfloat16

D_MODEL = 2048
DEPTH = 2
GRID_W = 64
CTX_LEN = 256
EPS = 1e-6
GROUP_W = D_MODEL // 4
RET_HEADS = 4
RET_DIM = GROUP_W // RET_HEADS
RET_CHUNK = 128
S5_GROUP = 16
S5_NG = GROUP_W // S5_GROUP
S5_P = 64
S5_STATE = S5_NG * S5_P
ATT_HEADS = 4
ATT_KV_HEADS = 2
ATT_DIM = GROUP_W // ATT_HEADS
ROPE_THETA = 10000.0
CONV_W = 3
N_EXPERTS = 32
TOP_K = 4
D_EXPERT = D_MODEL // 2
SWIGLU_LIMIT = 7.0
SWIGLU_ALPHA = 1.702

RET_OFF = 0
S5_OFF = 5 * GROUP_W
ATT_OFF = S5_OFF + GROUP_W
ATT_K_OFF = ATT_OFF + ATT_HEADS * ATT_DIM
ATT_V_OFF = ATT_K_OFF + ATT_KV_HEADS * ATT_DIM
CONV_OFF = ATT_V_OFF + ATT_KV_HEADS * ATT_DIM
IN_COLS = CONV_OFF + 3 * GROUP_W

LANES = 128
SUBLANES = 8
MIB = 1024 * 1024
NEG_BIG = -1e30


def _cparams(semantics, vmem_mib=48):
    return pltpu.CompilerParams(dimension_semantics=semantics,
                                vmem_limit_bytes=vmem_mib * MIB)


def _pick(n, prefs):
    for p in prefs:
        if n % p == 0:
            return p
    raise ValueError(f"no tile in {prefs} divides {n}")


def _dot(a, b):
    return jnp.dot(a, b, preferred_element_type=F32)


def _dot_nt(a, b):
    return lax.dot_general(a, b, (((1,), (1,)), ((), ())), preferred_element_type=F32)


def _dot_tn(a, b):
    return lax.dot_general(a, b, (((0,), (0,)), ((), ())), preferred_element_type=F32)


def _silu(x):
    return x * jax.nn.sigmoid(x)


def _mod_body(cc_ref, w_ref, b_ref, o_ref):
    a = _silu(cc_ref[...])
    o_ref[0] = _dot(a, w_ref[0]) + b_ref[0]


def _mod_call(cc, w_mod, b_mod):
    depth, d, n = w_mod.shape
    tn = 1536
    return pl.pallas_call(
        _mod_body,
        grid=(depth, n // tn),
        in_specs=[pl.BlockSpec((SUBLANES, d), lambda l, j: (0, 0)),
                  pl.BlockSpec((1, d, tn), lambda l, j: (l, 0, j)),
                  pl.BlockSpec((1, 1, tn), lambda l, j: (l, 0, j))],
        out_specs=pl.BlockSpec((1, SUBLANES, tn), lambda l, j: (l, 0, j)),
        out_shape=jax.ShapeDtypeStruct((depth, SUBLANES, n), F32),
        compiler_params=_cparams(("arbitrary", "arbitrary")),
        name="mod_vectors",
    )(cc, w_mod, b_mod.reshape(depth, 1, n))


def _row_mod(mod_ref, row0, rows, k):
    is_ctx = (row0 + lax.broadcasted_iota(jnp.int32, (rows, 1), 0)) < CTX_LEN
    return jnp.where(is_ctx, mod_ref[1, k:k + 1, :], mod_ref[0, k:k + 1, :])


def _inproj_body(x_ref, mod_ref, w_ref, o_ref, xn_ref, *, tm):
    i = pl.program_id(0)

    @pl.when(pl.program_id(1) == 0)
    def _():
        x = x_ref[...]
        xn = x * lax.rsqrt(jnp.mean(x * x, axis=-1, keepdims=True) + EPS)
        sh = _row_mod(mod_ref, i * tm, tm, 0)
        sc = _row_mod(mod_ref, i * tm, tm, 1)
        xn_ref[...] = (xn * (1.0 + sc) + sh).astype(BF16)

    o_ref[...] = _dot(xn_ref[...], w_ref[...]).astype(BF16)


def _inproj_call(xx, mod, w_in):
    t, d = xx.shape
    n = w_in.shape[1]
    tm = _pick(t, (768, 512, 256))
    tn = 512
    return pl.pallas_call(
        functools.partial(_inproj_body, tm=tm),
        grid=(t // tm, n // tn),
        in_specs=[pl.BlockSpec((tm, d), lambda i, j: (i, 0)),
                  pl.BlockSpec((2, 6, d), lambda i, j: (0, 0, 0)),
                  pl.BlockSpec((d, tn), lambda i, j: (0, j))],
        out_specs=pl.BlockSpec((tm, tn), lambda i, j: (i, j)),
        out_shape=jax.ShapeDtypeStruct((t, n), BF16),
        scratch_shapes=[pltpu.VMEM((tm, d), BF16)],
        compiler_params=_cparams(("arbitrary", "arbitrary")),
        name="inproj",
    )(xx, mod, w_in)


def _rope_angles(pos, dim):
    freqs = ROPE_THETA ** (-jnp.arange(0, dim, 2, dtype=F32) / dim)
    return pos.astype(F32)[:, None] * freqs[None, :]


def _rope_tables(seq_len):
    pos = jnp.arange(seq_len, dtype=jnp.int32)
    ang = _rope_angles(pos, RET_DIM)
    c_ret = jnp.concatenate([jnp.cos(ang)] * 2, axis=1)
    s_ret = jnp.concatenate([-jnp.sin(ang), jnp.sin(ang)], axis=1)
    rows = pos // GRID_W
    cols = pos % GRID_W
    ar = _rope_angles(rows, ATT_DIM // 2)
    ac = _rope_angles(cols, ATT_DIM // 2)
    c_att = jnp.concatenate([jnp.cos(ar)] * 2 + [jnp.cos(ac)] * 2, axis=1)
    s_att = jnp.concatenate([-jnp.sin(ar), jnp.sin(ar), -jnp.sin(ac), jnp.sin(ac)], axis=1)
    one = jnp.ones((CTX_LEN, LANES), F32)
    zero = jnp.zeros((CTX_LEN, LANES), F32)
    cat = lambda a, b: jnp.concatenate([a, b], axis=0)
    return cat(one, c_ret), cat(zero, s_ret), cat(one, c_att), cat(zero, s_att)


def _ret_tables(rc):
    lg = jnp.log(1.0 - 2.0 ** (-5.0 - jnp.arange(RET_HEADS, dtype=F32)))
    idx = jnp.arange(rc, dtype=F32)
    diff = idx[:, None] - idx[None, :]
    dec, zeta, xi, gch = [], [], [], []
    for direction, l in enumerate((lg, lg[::-1])):
        if direction == 0:
            dec.append(jnp.where(diff >= 0, jnp.exp(jnp.maximum(diff, 0.0)[None] * l[:, None, None]), 0.0))
            zeta.append(jnp.exp((rc - 1.0 - idx)[None, :] * l[:, None]))
            xi.append(jnp.exp((idx + 1.0)[None, :] * l[:, None]))
        else:
            dec.append(jnp.where(diff <= 0, jnp.exp(jnp.maximum(-diff, 0.0)[None] * l[:, None, None]), 0.0))
            zeta.append(jnp.exp(idx[None, :] * l[:, None]))
            xi.append(jnp.exp((rc - idx)[None, :] * l[:, None]))
        gch.append(jnp.exp(rc * l))
    dec = jnp.concatenate(dec, axis=0)
    bc = lambda a: jnp.broadcast_to(jnp.concatenate(a, axis=0)[:, :, None], (2 * RET_HEADS, rc, LANES))
    gch = jnp.broadcast_to(jnp.concatenate(gch, axis=0)[:, None, None], (2 * RET_HEADS, SUBLANES, LANES))
    return dec, bc(zeta), bc(xi), gch


def _rope_half(x, c, s):
    return x * c + pltpu.roll(x, RET_DIM // 2, axis=1) * s


def _group_norm(y):
    yc = y - jnp.mean(y, axis=-1, keepdims=True)
    return yc * lax.rsqrt(jnp.mean(yc * yc, axis=-1, keepdims=True) + EPS)


def _ret_body(qf, kf, vf, gf, cf, sf, qb, kb, vb, gb, cb, sb, dec_ref, zeta_ref, xi_ref, gch_ref,
              of_ref, ob_ref, state_ref):
    @pl.when(pl.program_id(0) == 0)
    def _():
        state_ref[...] = jnp.zeros_like(state_ref)

    dirs = ((qf, kf, vf, gf, cf, sf, of_ref), (qb, kb, vb, gb, cb, sb, ob_ref))
    for direction, (q_ref, k_ref, v_ref, g_ref, c_ref, s_ref, o_ref) in enumerate(dirs):
        c = c_ref[...]
        s = s_ref[...]
        for h in range(RET_HEADS):
            t = direction * RET_HEADS + h
            sl = slice(h * RET_DIM, (h + 1) * RET_DIM)
            q = _rope_half(q_ref[:, sl].astype(F32), c, s).astype(BF16)
            k = _rope_half(k_ref[:, sl].astype(F32), c, s) * RET_DIM ** -0.5
            v = v_ref[:, sl]
            state = state_ref[t]
            scores = _dot_nt(q, k.astype(BF16)) * dec_ref[t]
            y = _dot(scores.astype(BF16), v) + _dot(q, state.astype(BF16)) * xi_ref[t]
            kv = _dot_tn((k * zeta_ref[t]).astype(BF16), v)
            state_ref[t] = state * gch_ref[t][0:1, :] + kv
            g = g_ref[:, sl].astype(F32)
            o_ref[:, sl] = (_silu(g) * _group_norm(y)).astype(BF16)


def _ret_call(z, c_ret, s_ret, tables):
    t = z.shape[0]
    rc = RET_CHUNK
    nch = t // rc
    nctx = CTX_LEN // rc
    dec, zeta, xi, gch = tables

    def bwd(s):
        return jnp.where(s < nctx, nctx - 1 - s, nch + nctx - 1 - s)

    zf = lambda col: pl.BlockSpec((rc, GROUP_W), lambda s, col=col: (s, col))
    zb = lambda col: pl.BlockSpec((rc, GROUP_W), lambda s, col=col: (bwd(s), col))
    tf = pl.BlockSpec((rc, LANES), lambda s: (s, 0))
    tb = pl.BlockSpec((rc, LANES), lambda s: (bwd(s), 0))
    full = lambda a: pl.BlockSpec(a.shape, lambda s: (0,) * a.ndim)
    out = jax.ShapeDtypeStruct((t, GROUP_W), BF16)
    return pl.pallas_call(
        _ret_body,
        grid=(nch,),
        in_specs=[zf(0), zf(1), zf(2), zf(3), tf, tf, zb(0), zb(1), zb(2), zb(4), tb, tb,
                  full(dec), full(zeta), full(xi), full(gch)],
        out_specs=[pl.BlockSpec((rc, GROUP_W), lambda s: (s, 0)),
                   pl.BlockSpec((rc, GROUP_W), lambda s: (bwd(s), 0))],
        out_shape=[out, out],
        scratch_shapes=[pltpu.VMEM((2 * RET_HEADS, RET_DIM, RET_DIM), F32)],
        compiler_params=_cparams(("arbitrary",)),
        name="retention",
    )(z, z, z, z, c_ret, s_ret, z, z, z, z, c_ret, s_ret, dec, zeta, xi, gch)


S5_BLK = 4
S5_BLK_IN = GROUP_W // S5_BLK
S5_BLK_ST = S5_STATE // S5_BLK
S5_ROWS = SUBLANES
S5_NCOEF = 8


def _block_diag(w, rows_per_group, cols_per_group):
    gpb = S5_NG // S5_BLK
    w4 = w.reshape(S5_BLK, gpb, rows_per_group, cols_per_group)
    eye = jnp.eye(gpb, dtype=w.dtype)
    out = w4[:, :, :, None, :] * eye[None, :, None, :, None]
    return out.reshape(S5_BLK, gpb * rows_per_group, gpb * cols_per_group)


def _s5_prep_body(are_ref, aim_ref, ls_ref, wbr_ref, wbi_ref, coef_ref, wfr_ref, wfi_ref):
    a_re = are_ref[...]
    a_im = aim_ref[...]
    step = jnp.exp(ls_ref[...])
    mag = jnp.exp(a_re * step)
    ab_re = mag * jnp.cos(a_im * step)
    ab_im = mag * jnp.sin(a_im * step)
    den = a_re * a_re + a_im * a_im
    num_re = ab_re - 1.0
    f_re = (num_re * a_re + ab_im * a_im) / den
    f_im = (ab_im * a_re - num_re * a_im) / den
    row = lax.broadcasted_iota(jnp.int32, (S5_ROWS, S5_STATE), 0)
    for d in range(2):
        pr = [ab_re[d:d + 1]]
        pi = [ab_im[d:d + 1]]
        for _ in range(S5_ROWS - 1):
            pr.append(pr[-1] * ab_re[d:d + 1] - pi[-1] * ab_im[d:d + 1])
            pi.append(pr[-2] * ab_im[d:d + 1] + pi[-1] * ab_re[d:d + 1])
        zero = jnp.zeros((S5_ROWS, S5_STATE), F32)
        for n, shift in enumerate((1, 2, 4)):
            keep = (row >= shift) if d == 0 else (row < S5_ROWS - shift)
            coef_ref[d, 2 * n] = jnp.where(keep, pr[shift - 1], zero)
            coef_ref[d, 2 * n + 1] = jnp.where(keep, pi[shift - 1], zero)
        cr, ci = zero, zero
        for r in range(S5_ROWS):
            p = r if d == 0 else S5_ROWS - 1 - r
            cr = jnp.where(row == r, pr[p], cr)
            ci = jnp.where(row == r, pi[p], ci)
        coef_ref[d, 6] = cr
        coef_ref[d, 7] = ci
        for b in range(S5_BLK):
            cs = slice(b * S5_BLK_ST, (b + 1) * S5_BLK_ST)
            fr = f_re[d:d + 1, cs]
            fi = f_im[d:d + 1, cs]
            wfr_ref[d, b] = (wbr_ref[b] * fr - wbi_ref[b] * fi).astype(BF16)
            wfi_ref[d, b] = (wbr_ref[b] * fi + wbi_ref[b] * fr).astype(BF16)


def _s5_prep_call(a_re, a_im, log_step, b_re, b_im):
    flat = lambda a: a.astype(F32).reshape(2, S5_STATE)
    ls = jnp.repeat(log_step.astype(F32), S5_P, axis=-1)
    wbr = _block_diag(jnp.swapaxes(b_re.astype(F32), 1, 2), S5_GROUP, S5_P)
    wbi = _block_diag(jnp.swapaxes(b_im.astype(F32), 1, 2), S5_GROUP, S5_P)
    wshape = jax.ShapeDtypeStruct((2, S5_BLK, S5_BLK_IN, S5_BLK_ST), BF16)
    return pl.pallas_call(
        _s5_prep_body,
        out_shape=[jax.ShapeDtypeStruct((2, S5_NCOEF, S5_ROWS, S5_STATE), F32), wshape, wshape],
        compiler_params=pltpu.CompilerParams(vmem_limit_bytes=32 * MIB),
        name="s5_prep",
    )(flat(a_re), flat(a_im), ls, wbr, wbi)


def _s5_scan_body(uf_ref, ub_ref, coef_ref, wfr_ref, wfi_ref, wcr_ref, wci_ref, yf_ref, yb_ref,
                  dr_ref, di_ref, car_ref, *, tm):
    @pl.when(pl.program_id(0) == 0)
    def _():
        car_ref[...] = jnp.zeros_like(car_ref)

    nblk = tm // S5_ROWS
    for d, (u_ref, y_ref) in enumerate(((uf_ref, yf_ref), (ub_ref, yb_ref))):
        for b in range(S5_BLK):
            ub = u_ref[:, b * S5_BLK_IN:(b + 1) * S5_BLK_IN]
            cs = slice(b * S5_BLK_ST, (b + 1) * S5_BLK_ST)
            dr_ref[d, :, cs] = _dot(ub, wfr_ref[d, b])
            di_ref[d, :, cs] = _dot(ub, wfi_ref[d, b])

        def block(n, carry, d=d):
            c_re, c_im = carry
            blk = n if d == 0 else nblk - 1 - n
            r0 = pl.multiple_of(blk * S5_ROWS, S5_ROWS)
            xr = dr_ref[d, pl.ds(r0, S5_ROWS), :]
            xi = di_ref[d, pl.ds(r0, S5_ROWS), :]
            for n_s, shift in enumerate((1, 2, 4)):
                ar = coef_ref[d, 2 * n_s]
                ai = coef_ref[d, 2 * n_s + 1]
                amt = shift if d == 0 else S5_ROWS - shift
                sr = pltpu.roll(xr, amt, axis=0)
                si = pltpu.roll(xi, amt, axis=0)
                xr, xi = xr + ar * sr - ai * si, xi + ar * si + ai * sr
            pr = coef_ref[d, 6]
            pi = coef_ref[d, 7]
            xr, xi = xr + pr * c_re - pi * c_im, xi + pr * c_im + pi * c_re
            dr_ref[d, pl.ds(r0, S5_ROWS), :] = xr
            di_ref[d, pl.ds(r0, S5_ROWS), :] = xi
            last = S5_ROWS - 1 if d == 0 else 0
            return xr[last:last + 1, :], xi[last:last + 1, :]

        c_re, c_im = lax.fori_loop(0, nblk, block, (car_ref[d, 0:1, :], car_ref[d, 1:2, :]))
        car_ref[d, 0:1, :] = c_re
        car_ref[d, 1:2, :] = c_im
        for b in range(S5_BLK):
            cs = slice(b * S5_BLK_ST, (b + 1) * S5_BLK_ST)
            y_ref[:, b * S5_BLK_IN:(b + 1) * S5_BLK_IN] = (
                _dot(dr_ref[d, :, cs].astype(BF16), wcr_ref[b]) - _dot(di_ref[d, :, cs].astype(BF16), wci_ref[b]))


def _s5_scan_call(z, coef, wfr, wfi, wcr, wci):
    t = z.shape[0]
    tm = CTX_LEN
    nt = t // tm
    nctx = CTX_LEN // tm
    ucol = S5_OFF // GROUP_W

    def bwd(s):
        return jnp.where(s < nctx, nctx - 1 - s, nt + nctx - 1 - s)

    full = lambda a: pl.BlockSpec(a.shape, lambda s: (0,) * a.ndim)
    out = jax.ShapeDtypeStruct((t, GROUP_W), F32)
    return pl.pallas_call(
        functools.partial(_s5_scan_body, tm=tm),
        grid=(nt,),
        in_specs=[pl.BlockSpec((tm, GROUP_W), lambda s: (s, ucol)),
                  pl.BlockSpec((tm, GROUP_W), lambda s: (bwd(s), ucol)),
                  full(coef), full(wfr), full(wfi), full(wcr), full(wci)],
        out_specs=[pl.BlockSpec((tm, GROUP_W), lambda s: (s, 0)),
                   pl.BlockSpec((tm, GROUP_W), lambda s: (bwd(s), 0))],
        out_shape=[out, out],
        scratch_shapes=[pltpu.VMEM((2, tm, S5_STATE), F32), pltpu.VMEM((2, tm, S5_STATE), F32),
                        pltpu.VMEM((2, SUBLANES, S5_STATE), F32)],
        compiler_params=_cparams(("arbitrary",)),
        name="s5_scan",
    )(z, z, coef, wfr, wfi, wcr, wci)


def _gelu_tanh(x):
    return 0.5 * x * (1.0 + jnp.tanh(math.sqrt(2.0 / math.pi) * (x + 0.044715 * (x * x * x))))


def _s5_finish_body(yf_ref, yb_ref, u_ref, d_ref, w_ref, b_ref, o_ref):
    y = yf_ref[...] + yb_ref[...] + u_ref[...].astype(F32) * d_ref[...]
    y = _gelu_tanh(y)
    o_ref[...] = (y * jax.nn.sigmoid(_dot(y.astype(BF16), w_ref[...]) + b_ref[...])).astype(BF16)


def _s5_finish_call(yf, yb, z, d_skip, w_glu, b_glu):
    t = z.shape[0]
    tm = _pick(t, (768, 512, 256))
    ucol = S5_OFF // GROUP_W
    row = pl.BlockSpec((tm, GROUP_W), lambda i: (i, 0))
    vec = pl.BlockSpec((1, GROUP_W), lambda i: (0, 0))
    return pl.pallas_call(
        _s5_finish_body,
        grid=(t // tm,),
        in_specs=[row, row, pl.BlockSpec((tm, GROUP_W), lambda i: (i, ucol)), vec,
                  pl.BlockSpec((GROUP_W, GROUP_W), lambda i: (0, 0)), vec],
        out_specs=row,
        out_shape=jax.ShapeDtypeStruct((t, GROUP_W), BF16),
        compiler_params=_cparams(("arbitrary",)),
        name="s5_finish",
    )(yf, yb, z, d_skip.astype(F32).reshape(1, GROUP_W), w_glu.astype(BF16),
      b_glu.astype(F32).reshape(1, GROUP_W))


def _head_rms_rope(x, w, c, s, lane):
    xn = x * lax.rsqrt(jnp.mean(x * x, axis=-1, keepdims=True) + EPS) * w
    quarter = ATT_DIM // 4
    first = (lane % (2 * quarter)) < quarter
    partner = jnp.where(first, pltpu.roll(xn, ATT_DIM - quarter, axis=1), pltpu.roll(xn, quarter, axis=1))
    return xn * c + partner * s


LOG2E = math.log2(math.e)
ATT_KEY_CHUNK_PREFS = (768, 512, 256)


def _att_prep_body(q_ref, k_ref, v_ref, c_ref, s_ref, qw_ref, kw_ref, qo_ref, ko_ref, vo_ref):
    c = c_ref[...]
    s = s_ref[...]
    lane = lax.broadcasted_iota(jnp.int32, c.shape, 1)
    for h in range(ATT_HEADS):
        sl = slice(h * ATT_DIM, (h + 1) * ATT_DIM)
        q = _head_rms_rope(q_ref[:, sl].astype(F32), qw_ref[...], c, s, lane)
        qo_ref[sl, :] = (q * (ATT_DIM ** -0.5 * LOG2E)).T.astype(BF16)
    for h in range(ATT_KV_HEADS):
        sl = slice(h * ATT_DIM, (h + 1) * ATT_DIM)
        ko_ref[:, sl] = _head_rms_rope(k_ref[:, sl].astype(F32), kw_ref[...], c, s, lane).astype(BF16)
        vo_ref[h, 0] = v_ref[:, sl].astype(F32).T.astype(BF16)


def _att_prep_call(z, c_att, s_att, q_norm_w, k_norm_w):
    t = z.shape[0]
    tm = _pick(t, ATT_KEY_CHUNK_PREFS)
    qw, kw = ATT_HEADS * ATT_DIM, ATT_KV_HEADS * ATT_DIM
    tab = pl.BlockSpec((tm, LANES), lambda i: (i, 0))
    vec = pl.BlockSpec((1, ATT_DIM), lambda i: (0, 0))
    return pl.pallas_call(
        _att_prep_body,
        grid=(t // tm,),
        in_specs=[pl.BlockSpec((tm, qw), lambda i: (i, ATT_OFF // qw)),
                  pl.BlockSpec((tm, kw), lambda i: (i, ATT_K_OFF // kw)),
                  pl.BlockSpec((tm, kw), lambda i: (i, ATT_V_OFF // kw)), tab, tab, vec, vec],
        out_specs=[pl.BlockSpec((qw, tm), lambda i: (0, i)), pl.BlockSpec((tm, kw), lambda i: (i, 0)),
                   pl.BlockSpec((ATT_KV_HEADS, 1, ATT_DIM, tm), lambda i: (0, i, 0, 0))],
        out_shape=[jax.ShapeDtypeStruct((qw, t), BF16), jax.ShapeDtypeStruct((t, kw), BF16),
                   jax.ShapeDtypeStruct((ATT_KV_HEADS, t // tm, ATT_DIM, tm), BF16)],
        compiler_params=_cparams(("arbitrary",)),
        name="att_prep",
    )(z, z, z, c_att, s_att, q_norm_w.astype(F32).reshape(1, ATT_DIM), k_norm_w.astype(F32).reshape(1, ATT_DIM))


ATT_SUB = 128


def _flash_body(q0_ref, q1_ref, k_ref, v_ref, o_ref, p_ref, m_ref, l_ref, acc_ref, *, tq, tk, n_lat):
    q_refs = (q0_ref, q1_ref)
    grp = len(q_refs)
    m_ref[...] = jnp.full_like(m_ref, NEG_BIG)
    l_ref[...] = jnp.zeros_like(l_ref)
    acc_ref[...] = jnp.zeros_like(acc_ref)

    def chunk(c, keys):
        nsub = keys // ATT_SUB

        def scores(g, j):
            r0 = c * tk + j * ATT_SUB
            r0 = r0 if isinstance(r0, int) else pl.multiple_of(r0, ATT_SUB)
            return _dot(k_ref[pl.ds(r0, ATT_SUB), :], q_refs[g][...])

        m_old = m_ref[0:1, :]
        m_new = []
        for g in range(grp):
            mg = m_old[:, g * tq:(g + 1) * tq]
            for j in range(nsub):
                mg = jnp.maximum(mg, jnp.max(scores(g, j), axis=0, keepdims=True))
            m_new.append(mg)
        m_ref[0:1, :] = jnp.concatenate(m_new, axis=1)
        m_cat = m_ref[0:1, :]
        alpha = jnp.exp2(m_old - m_cat)
        l_add = []
        for g in range(grp):
            mg = m_cat[:, g * tq:(g + 1) * tq]
            lg = jnp.zeros((1, tq), F32)
            for j in range(nsub):
                p = jnp.exp2(scores(g, j) - mg)
                lg = lg + jnp.sum(p, axis=0, keepdims=True)
                p_ref[j * ATT_SUB:(j + 1) * ATT_SUB, g * tq:(g + 1) * tq] = p.astype(BF16)
            l_add.append(lg)
        l_ref[0:1, :] = alpha * l_ref[0:1, :] + jnp.concatenate(l_add, axis=1)
        if keys == tk:
            pv = _dot(v_ref[0, c], p_ref[...])
        else:
            pv = _dot(v_ref[0, 0, :, 0:keys], p_ref[0:keys, :])
        acc_ref[...] = acc_ref[...] * alpha + pv

    is_ctx = pl.program_id(1) < CTX_LEN // tq

    @pl.when(is_ctx)
    def _():
        chunk(0, CTX_LEN)

    @pl.when(jnp.logical_not(is_ctx))
    def _():
        def step(c, carry):
            chunk(c, tk)
            return carry
        lax.fori_loop(0, n_lat, step, 0)

    o = acc_ref[...] / l_ref[0:1, :]
    for g in range(grp):
        o_ref[:, g * ATT_DIM:(g + 1) * ATT_DIM] = o[:, g * tq:(g + 1) * tq].T.astype(BF16)


def _flash_call(qt, kn, vt):
    t = kn.shape[0]
    tq = CTX_LEN
    n_chunks, tk = vt.shape[1], vt.shape[3]
    grp = ATT_HEADS // ATT_KV_HEADS
    assert grp == 2
    qspec = lambda g: pl.BlockSpec((ATT_DIM, tq), lambda h, i, g=g: (grp * h + g, i))
    return pl.pallas_call(
        functools.partial(_flash_body, tq=tq, tk=tk, n_lat=n_chunks),
        grid=(ATT_KV_HEADS, t // tq),
        in_specs=[qspec(0), qspec(1),
                  pl.BlockSpec((t, ATT_DIM), lambda h, i: (0, h)),
                  pl.BlockSpec((1, n_chunks, ATT_DIM, tk), lambda h, i: (h, 0, 0, 0))],
        out_specs=pl.BlockSpec((tq, grp * ATT_DIM), lambda h, i: (i, h)),
        out_shape=jax.ShapeDtypeStruct((t, ATT_HEADS * ATT_DIM), BF16),
        scratch_shapes=[pltpu.VMEM((tk, grp * tq), BF16),
                        pltpu.VMEM((SUBLANES, grp * tq), F32), pltpu.VMEM((SUBLANES, grp * tq), F32),
                        pltpu.VMEM((ATT_DIM, grp * tq), F32)],
        compiler_params=_cparams(("arbitrary", "arbitrary")),
        name="flash_attention",
    )(qt, qt, kn, vt)


def _conv_body(b_ref, c_ref, h_ref, w_ref, o_ref):
    t = b_ref.shape[0]
    p = c_ref[...].astype(F32) * h_ref[...].astype(F32)
    row = lax.broadcasted_iota(jnp.int32, p.shape, 0)
    seq_first = (row == 0) | (row == CTX_LEN)
    seq_last = (row == CTX_LEN - 1) | (row == t - 1)
    prev = jnp.where(seq_first, 0.0, pltpu.roll(p, 1, axis=0))
    nxt = jnp.where(seq_last, 0.0, pltpu.roll(p, t - 1, axis=0))
    w = w_ref[...]
    y = w[0:1, :] * prev + w[1:2, :] * p + w[2:3, :] * nxt
    o_ref[...] = (b_ref[...].astype(F32) * y).astype(BF16)


def _conv_call(z, conv_w):
    t = z.shape[0]
    col0 = CONV_OFF // LANES
    nblk = GROUP_W // LANES
    zc = lambda k: pl.BlockSpec((t, LANES), lambda j, k=k: (0, col0 + k * nblk + j))
    return pl.pallas_call(
        _conv_body,
        grid=(nblk,),
        in_specs=[zc(0), zc(1), zc(2), pl.BlockSpec((CONV_W, LANES), lambda j: (0, j))],
        out_specs=pl.BlockSpec((t, LANES), lambda j: (0, j)),
        out_shape=jax.ShapeDtypeStruct((t, GROUP_W), BF16),
        compiler_params=_cparams(("arbitrary",)),
        name="gated_conv",
    )(z, z, z, conv_w.astype(F32))


ROUTE_IDX, ROUTE_GATE, ROUTE_RANK = 0, TOP_K, 2 * TOP_K


def _outproj_body(rf_ref, rb_ref, s_ref, a_ref, v_ref, x_ref, mod_ref, w_ref, wr_ref, br_ref, tri_ref,
                  xo_ref, fo_ref, ro_ref, cnt_ref, count_ref, *, tm):
    i = pl.program_id(0)

    @pl.when(i == 0)
    def _():
        count_ref[...] = jnp.zeros_like(count_ref)

    r = (rf_ref[...].astype(F32) + rb_ref[...].astype(F32)).astype(BF16)
    g = GROUP_W
    mix = (_dot(r, w_ref[0:g, :]) + _dot(s_ref[...], w_ref[g:2 * g, :])
           + _dot(a_ref[...], w_ref[2 * g:3 * g, :]) + _dot(v_ref[...], w_ref[3 * g:4 * g, :]))
    xn = x_ref[...] + _row_mod(mod_ref, i * tm, tm, 2) * mix
    xo_ref[...] = xn
    f = xn * lax.rsqrt(jnp.mean(xn * xn, axis=-1, keepdims=True) + EPS)
    f = f * (1.0 + _row_mod(mod_ref, i * tm, tm, 4)) + _row_mod(mod_ref, i * tm, tm, 3)
    fo_ref[...] = f

    fh = f.astype(BF16)
    fl = (f - fh.astype(F32)).astype(BF16)
    d = f.shape[1]
    logits = (_dot(fh, wr_ref[0:d, :]) + _dot(fl, wr_ref[0:d, :]) + _dot(fh, wr_ref[d:2 * d, :])
              + br_ref[...])
    lane = lax.broadcasted_iota(jnp.int32, logits.shape, 1).astype(F32)
    vals = logits
    top_v, top_i, hits = [], [], []
    for _ in range(TOP_K):
        mk = jnp.max(vals, axis=1, keepdims=True)
        ik = jnp.min(jnp.where(vals == mk, lane, float(LANES)), axis=1, keepdims=True)
        hit = lane == ik
        top_v.append(mk)
        top_i.append(ik)
        hits.append(hit)
        vals = jnp.where(hit, 2.0 * NEG_BIG, vals)
    ex = [jnp.exp(v - top_v[0]) for v in top_v]
    den = ex[0] + ex[1] + ex[2] + ex[3]
    member = jnp.zeros_like(logits)
    for hit in hits:
        member = jnp.where(hit, 1.0, member)
    ahead = _dot(tri_ref[...], member.astype(BF16)) + count_ref[0:1, :]
    record = jnp.zeros_like(logits)
    for k in range(TOP_K):
        rank = jnp.sum(jnp.where(hits[k], ahead, 0.0), axis=1, keepdims=True)
        record = jnp.where(lane == float(ROUTE_IDX + k), top_i[k], record)
        record = jnp.where(lane == float(ROUTE_GATE + k), ex[k] / den, record)
        record = jnp.where(lane == float(ROUTE_RANK + k), rank, record)
    ro_ref[...] = record
    count_ref[...] = count_ref[...] + jnp.sum(member, axis=0, keepdims=True)
    cnt_ref[...] = count_ref[...]


def _outproj_call(rf, rb, s5, att, cv, xx, mod, w_out, w_router, b_router):
    t, d = xx.shape
    tm = CTX_LEN
    wr = jnp.zeros((d, LANES), F32).at[:, :N_EXPERTS].set(w_router.astype(F32))
    wr_hi = wr.astype(BF16)
    wr_lo = (wr - wr_hi.astype(F32)).astype(BF16)
    wr2 = jnp.concatenate([wr_hi, wr_lo], axis=0)
    br = jnp.full((1, LANES), NEG_BIG, F32).at[0, :N_EXPERTS].set(b_router.astype(F32))
    tri = jnp.tri(tm, k=-1, dtype=BF16)
    mix = pl.BlockSpec((tm, GROUP_W), lambda i: (i, 0))
    row = pl.BlockSpec((tm, d), lambda i: (i, 0))
    full = lambda a: pl.BlockSpec(a.shape, lambda i: (0,) * a.ndim)
    rec = pl.BlockSpec((tm, LANES), lambda i: (i, 0))
    return pl.pallas_call(
        functools.partial(_outproj_body, tm=tm),
        grid=(t // tm,),
        in_specs=[mix, mix, mix, mix, mix, row, full(mod), full(w_out), full(wr2), full(br), full(tri)],
        out_specs=[row, row, rec, pl.BlockSpec((SUBLANES, LANES), lambda i: (0, 0))],
        out_shape=[jax.ShapeDtypeStruct((t, d), F32), jax.ShapeDtypeStruct((t, d), F32),
                   jax.ShapeDtypeStruct((t, LANES), F32), jax.ShapeDtypeStruct((SUBLANES, LANES), F32)],
        scratch_shapes=[pltpu.VMEM((SUBLANES, LANES), F32)],
        compiler_params=_cparams(("arbitrary",)),
        name="outproj_route",
    )(rf, rb, s5, att, cv, xx, mod, w_out, wr2, br, tri)


MXU_TILE = 256
WPREP_ROWS = 512


def _deinterleave_matrix():
    src = jnp.arange(MXU_TILE)
    dst = jnp.where(src % 2 == 0, src // 2, MXU_TILE // 2 + src // 2)
    return (dst[:, None] == jnp.arange(MXU_TILE)[None, :]).astype(BF16)


def _gate_up_prep_body(w_ref, p_ref, o_ref):
    half = MXU_TILE // 2
    for c in range(2 * D_EXPERT // MXU_TILE):
        blk = w_ref[0, 0, :, c * MXU_TILE:(c + 1) * MXU_TILE].astype(BF16)
        r = _dot(blk, p_ref[...])
        o_ref[0, :, c * half:(c + 1) * half] = r[:, :half].astype(BF16)
        o_ref[0, :, D_EXPERT + c * half:D_EXPERT + (c + 1) * half] = r[:, half:].astype(BF16)


def _gate_up_prep_call(w_gate_up, layer):
    _, n_e, d, n = w_gate_up.shape
    perm = _deinterleave_matrix()
    return pl.pallas_call(
        _gate_up_prep_body,
        grid=(n_e, d // WPREP_ROWS),
        in_specs=[pl.BlockSpec((1, 1, WPREP_ROWS, n), lambda e, r: (layer, e, r, 0)),
                  pl.BlockSpec(perm.shape, lambda e, r: (0, 0))],
        out_specs=pl.BlockSpec((1, WPREP_ROWS, n), lambda e, r: (e, r, 0)),
        out_shape=jax.ShapeDtypeStruct((n_e, d, n), BF16),
        compiler_params=_cparams(("arbitrary", "arbitrary")),
        name="gate_up_prep",
    )(w_gate_up, perm)


def _cast_body(w_ref, o_ref):
    o_ref[0] = w_ref[0, 0].astype(BF16)


def _down_prep_call(w_down, layer):
    _, n_e, k, n = w_down.shape
    return pl.pallas_call(
        _cast_body,
        grid=(n_e, k // WPREP_ROWS),
        in_specs=[pl.BlockSpec((1, 1, WPREP_ROWS, n), lambda e, r: (layer, e, r, 0))],
        out_specs=pl.BlockSpec((1, WPREP_ROWS, n), lambda e, r: (e, r, 0)),
        out_shape=jax.ShapeDtypeStruct((n_e, k, n), BF16),
        compiler_params=_cparams(("arbitrary", "arbitrary")),
        name="down_prep",
    )(w_down)


SLOT_BLOCK = 256
GATHER_UNROLL = 8


def _row_copy(src_hbm, row, dst, dst_row, sem):
    return pltpu.make_async_copy(src_hbm.at[pl.ds(row, 1), :], dst.at[pl.ds(dst_row, 1), :], sem)


def _moe_body(tok_ref, be_ref, nu_ref, f_hbm, wgu_ref, wd_ref, bgu_ref, bd_ref, o_ref, xbuf, sem):
    b = pl.program_id(0)
    n_used = nu_ref[0]

    def issue(blk, slot):
        def some(n, carry):
            for u in range(GATHER_UNROLL):
                r = n * GATHER_UNROLL + u
                _row_copy(f_hbm, tok_ref[blk * SLOT_BLOCK + r], xbuf.at[slot], r, sem.at[slot]).start()
            return carry
        lax.fori_loop(0, SLOT_BLOCK // GATHER_UNROLL, some, 0)

    def wait(slot):
        pltpu.make_async_copy(f_hbm.at[pl.ds(0, SLOT_BLOCK), :], xbuf.at[slot], sem.at[slot]).wait()

    slot = b % 2

    @pl.when((b == 0) & (n_used > 0))
    def _():
        issue(0, 0)

    @pl.when(b < n_used)
    def _():
        wait(slot)

        @pl.when(b + 1 < n_used)
        def _():
            issue(b + 1, 1 - slot)

        x = xbuf[slot].astype(BF16)
        gu = _dot(x, wgu_ref[0]) + bgu_ref[0]
        gate = jnp.minimum(gu[:, :D_EXPERT], SWIGLU_LIMIT)
        up = jnp.clip(gu[:, D_EXPERT:], -SWIGLU_LIMIT, SWIGLU_LIMIT)
        act = (up + 1.0) * gate * jax.nn.sigmoid(gate * SWIGLU_ALPHA)
        o_ref[...] = _dot(act.astype(BF16), wd_ref[0]) + bd_ref[0]

    @pl.when(b >= n_used)
    def _():
        o_ref[...] = jnp.zeros_like(o_ref)


def _moe_call(slot_tok, block_e, n_used, f, wgu, wd, bgu, bd):
    n_blocks = block_e.shape[0]
    d = f.shape[1]
    e_spec = lambda shape: pl.BlockSpec((1,) + shape, lambda b, tok, be, nu: (be[b], 0, 0))
    grid_spec = pltpu.PrefetchScalarGridSpec(
        num_scalar_prefetch=3,
        grid=(n_blocks,),
        in_specs=[pl.BlockSpec(memory_space=pl.ANY),
                  e_spec((d, 2 * D_EXPERT)), e_spec((D_EXPERT, d)),
                  e_spec((1, 2 * D_EXPERT)), e_spec((1, d))],
        out_specs=pl.BlockSpec((SLOT_BLOCK, d), lambda b, tok, be, nu: (b, 0)),
        scratch_shapes=[pltpu.VMEM((2, SLOT_BLOCK, d), F32), pltpu.SemaphoreType.DMA((2,))],
    )
    return pl.pallas_call(
        _moe_body,
        grid_spec=grid_spec,
        out_shape=jax.ShapeDtypeStruct((n_blocks * SLOT_BLOCK, d), F32),
        compiler_params=_cparams(("arbitrary",)),
        name="moe_experts",
    )(slot_tok, block_e, n_used, f, wgu, wd, bgu, bd)


COMBINE_TOKENS = 64


def _combine_body(dest_ref, y_hbm, gate_ref, x_ref, mod_ref, o_ref, buf, sem):
    i = pl.program_id(0)
    n = pl.num_programs(0)
    tt = COMBINE_TOKENS

    def issue(tile, slot):
        def one(r, carry):
            for k in range(TOP_K):
                _row_copy(y_hbm, dest_ref[(tile * tt + r) * TOP_K + k], buf.at[slot, k], r, sem.at[slot]).start()
            return carry
        lax.fori_loop(0, tt, one, 0)

    def wait(slot):
        def one(r, carry):
            for k in range(TOP_K):
                _row_copy(y_hbm, 0, buf.at[slot, k], r, sem.at[slot]).wait()
            return carry
        lax.fori_loop(0, tt, one, 0)

    slot = i % 2

    @pl.when(i == 0)
    def _():
        issue(0, 0)

    wait(slot)

    @pl.when(i + 1 < n)
    def _():
        issue(i + 1, 1 - slot)

    gates = gate_ref[...]
    y = buf[slot, 0] * gates[:, 0:1]
    for k in range(1, TOP_K):
        y = y + buf[slot, k] * gates[:, k:k + 1]
    o_ref[...] = x_ref[...] + mod_ref[0, 5:6, :] * y


def _combine_call(dest, yb, gates, xnew, mod):
    t, d = xnew.shape
    tt = COMBINE_TOKENS
    nctx = CTX_LEN // tt
    grid_spec = pltpu.PrefetchScalarGridSpec(
        num_scalar_prefetch=1,
        grid=(t // tt,),
        in_specs=[pl.BlockSpec(memory_space=pl.ANY),
                  pl.BlockSpec((tt, TOP_K), lambda i, dest: (i, 0)),
                  pl.BlockSpec((tt, d), lambda i, dest: (i, 0)),
                  pl.BlockSpec((1, 6, d), lambda i, dest: (jnp.where(i < nctx, 1, 0), 0, 0))],
        out_specs=pl.BlockSpec((tt, d), lambda i, dest: (i, 0)),
        scratch_shapes=[pltpu.VMEM((2, TOP_K, tt, d), F32), pltpu.SemaphoreType.DMA((2,))],
    )
    return pl.pallas_call(
        _combine_body,
        grid_spec=grid_spec,
        out_shape=jax.ShapeDtypeStruct((t, d), F32),
        compiler_params=_cparams(("arbitrary",)),
        name="moe_combine",
    )(dest, yb, gates, xnew, mod)


def _routing_tables(record, counts, t):
    n_blocks = -(-(t * TOP_K + N_EXPERTS * (SLOT_BLOCK - 1)) // SLOT_BLOCK)
    top_i = record[:, ROUTE_IDX:ROUTE_IDX + TOP_K].astype(jnp.int32)
    gates = record[:, ROUTE_GATE:ROUTE_GATE + TOP_K]
    rank = record[:, ROUTE_RANK:ROUTE_RANK + TOP_K].astype(jnp.int32)
    cnt = counts[0, :N_EXPERTS].astype(jnp.int32)
    padded = (cnt + SLOT_BLOCK - 1) // SLOT_BLOCK * SLOT_BLOCK
    pend = jnp.cumsum(padded)
    pstart = pend - padded
    dest = (pstart[top_i] + rank).reshape(-1)
    flat_tok = jnp.repeat(jnp.arange(t, dtype=jnp.int32), TOP_K)
    slot_tok = jnp.zeros((n_blocks * SLOT_BLOCK,), jnp.int32).at[dest].set(flat_tok)
    block_start = jnp.arange(n_blocks, dtype=jnp.int32) * SLOT_BLOCK
    block_e = jnp.minimum(jnp.sum((pend[None, :] <= block_start[:, None]).astype(jnp.int32), axis=1),
                          N_EXPERTS - 1).astype(jnp.int32)
    n_used = (pend[-1:] // SLOT_BLOCK).astype(jnp.int32)
    return dest.astype(jnp.int32), gates, slot_tok, block_e, n_used


def kernel(x, c, ctx, c_ctx, w_mod, b_mod, w_in, w_out, s5_a_re, s5_a_im, s5_log_step, s5_b_re, s5_b_im,
           s5_c_re, s5_c_im, s5_d, s5_w_glu, s5_b_glu, q_norm_w, k_norm_w, conv_w, w_router, b_router,
           w_gate_up, b_gate_up, w_down, b_down):
    bsz, seq_len, d = x.shape
    assert bsz == 1 and d == D_MODEL and ctx.shape[1] == CTX_LEN
    t = CTX_LEN + seq_len
    xx = jnp.concatenate([ctx[0], x[0]], axis=0).astype(F32)

    cc = jnp.zeros((SUBLANES, d), F32).at[0].set(c[0]).at[1].set(c_ctx)
    mods = _mod_call(cc, w_mod.astype(F32), b_mod.astype(F32))[:, :2].reshape(DEPTH, 2, 6, d)
    c_ret, s_ret, c_att, s_att = _rope_tables(seq_len)
    ret_tables = _ret_tables(RET_CHUNK)

    for l in range(DEPTH):
        mod = mods[l]
        z = _inproj_call(xx, mod, w_in[l].astype(BF16))
        rf, rb = _ret_call(z, c_ret, s_ret, ret_tables)
        coef, wfr, wfi = _s5_prep_call(s5_a_re[l], s5_a_im[l], s5_log_step[l], s5_b_re[l], s5_b_im[l])
        wcr = _block_diag(jnp.swapaxes(s5_c_re[l].astype(F32), 1, 2), S5_P, S5_GROUP).astype(BF16)
        wci = _block_diag(jnp.swapaxes(s5_c_im[l].astype(F32), 1, 2), S5_P, S5_GROUP).astype(BF16)
        yf, yb = _s5_scan_call(z, coef, wfr, wfi, wcr, wci)
        s5 = _s5_finish_call(yf, yb, z, s5_d[l], s5_w_glu[l], s5_b_glu[l])
        qt, kn, vt = _att_prep_call(z, c_att, s_att, q_norm_w[l], k_norm_w[l])
        att = _flash_call(qt, kn, vt)
        cv = _conv_call(z, conv_w[l])
        xnew, f, record, counts = _outproj_call(rf, rb, s5, att, cv, xx, mod, w_out[l].astype(BF16),
                                                w_router[l], b_router[l])
        dest, gates, slot_tok, block_e, n_used = _routing_tables(record, counts, t)
        wgu = _gate_up_prep_call(w_gate_up, l)
        wd = _down_prep_call(w_down, l)
        bgu = jnp.concatenate([b_gate_up[l][:, 0::2], b_gate_up[l][:, 1::2]], axis=1)[:, None, :].astype(F32)
        y_slots = _moe_call(slot_tok, block_e, n_used, f, wgu, wd, bgu, b_down[l][:, None, :].astype(F32))
        xx = _combine_call(dest, y_slots, gates, xnew, mod)

    return xx[CTX_LEN:][None].astype(x.dtype)
```

```python
import functools
import math

import jax
import jax.numpy as jnp
from jax import lax
from jax.experimental import pallas as pl
from jax.experimental.pallas import tpu as pltpu

F32 = jnp.float32
BF16 = jnp.bfloat16

D_MODEL = 2048
DEPTH = 2
GRID_W = 64
CTX_LEN = 256
EPS = 1e-6
GROUP_W = D_MODEL // 4
RET_HEADS = 4
RET_DIM = GROUP_W // RET_HEADS
RET_CHUNK = 128
S5_GROUP = 16
S5_NG = GROUP_W // S5_GROUP
S5_P = 64
S5_STATE = S5_NG * S5_P
ATT_HEADS = 4
ATT_KV_HEADS = 2
ATT_DIM = GROUP_W // ATT_HEADS
ROPE_THETA = 10000.0
CONV_W = 3
N_EXPERTS = 32
TOP_K = 4
D_EXPERT = D_MODEL // 2
SWIGLU_LIMIT = 7.0
SWIGLU_ALPHA = 1.702

RET_OFF = 0
S5_OFF = 5 * GROUP_W
ATT_OFF = S5_OFF + GROUP_W
ATT_K_OFF = ATT_OFF + ATT_HEADS * ATT_DIM
ATT_V_OFF = ATT_K_OFF + ATT_KV_HEADS * ATT_DIM
CONV_OFF = ATT_V_OFF + ATT_KV_HEADS * ATT_DIM
IN_COLS = CONV_OFF + 3 * GROUP_W

LANES = 128
SUBLANES = 8
MIB = 1024 * 1024
NEG_BIG = -1e30


def _cparams(semantics, vmem_mib=48):
    return pltpu.CompilerParams(dimension_semantics=semantics,
                                vmem_limit_bytes=vmem_mib * MIB)


def _pick(n, prefs):
    for p in prefs:
        if n % p == 0:
            return p
    raise ValueError(f"no tile in {prefs} divides {n}")


def _dot(a, b):
    return jnp.dot(a, b, preferred_element_type=F32)


def _dot_nt(a, b):
    return lax.dot_general(a, b, (((1,), (1,)), ((), ())), preferred_element_type=F32)


def _dot_tn(a, b):
    return lax.dot_general(a, b, (((0,), (0,)), ((), ())), preferred_element_type=F32)


def _silu(x):
    return x * jax.nn.sigmoid(x)


def _mod_body(cc_ref, w_ref, b_ref, o_ref):
    a = _silu(cc_ref[...])
    o_ref[0] = _dot(a, w_ref[0]) + b_ref[0]


def _mod_call(cc, w_mod, b_mod):
    depth, d, n = w_mod.shape
    tn = 1536
    return pl.pallas_call(
        _mod_body,
        grid=(depth, n // tn),
        in_specs=[pl.BlockSpec((SUBLANES, d), lambda l, j: (0, 0)),
                  pl.BlockSpec((1, d, tn), lambda l, j: (l, 0, j)),
                  pl.BlockSpec((1, 1, tn), lambda l, j: (l, 0, j))],
        out_specs=pl.BlockSpec((1, SUBLANES, tn), lambda l, j: (l, 0, j)),
        out_shape=jax.ShapeDtypeStruct((depth, SUBLANES, n), F32),
        compiler_params=_cparams(("arbitrary", "arbitrary")),
        name="mod_vectors",
    )(cc, w_mod, b_mod.reshape(depth, 1, n))


def _row_mod(mod_ref, row0, rows, k):
    is_ctx = (row0 + lax.broadcasted_iota(jnp.int32, (rows, 1), 0)) < CTX_LEN
    return jnp.where(is_ctx, mod_ref[1, k:k + 1, :], mod_ref[0, k:k + 1, :])


def _inproj_body(x_ref, mod_ref, w_ref, o_ref, xn_ref, *, tm):
    i = pl.program_id(0)

    @pl.when(pl.program_id(1) == 0)
    def _():
        x = x_ref[...]
        xn = x * lax.rsqrt(jnp.mean(x * x, axis=-1, keepdims=True) + EPS)
        sh = _row_mod(mod_ref, i * tm, tm, 0)
        sc = _row_mod(mod_ref, i * tm, tm, 1)
        xn_ref[...] = (xn * (1.0 + sc) + sh).astype(BF16)

    o_ref[...] = _dot(xn_ref[...], w_ref[...]).astype(BF16)


def _inproj_call(xx, mod, w_in):
    t, d = xx.shape
    n = w_in.shape[1]
    tm = _pick(t, (768, 512, 256))
    tn = 512
    return pl.pallas_call(
        functools.partial(_inproj_body, tm=tm),
        grid=(t // tm, n // tn),
        in_specs=[pl.BlockSpec((tm, d), lambda i, j: (i, 0)),
                  pl.BlockSpec((2, 6, d), lambda i, j: (0, 0, 0)),
                  pl.BlockSpec((d, tn), lambda i, j: (0, j))],
        out_specs=pl.BlockSpec((tm, tn), lambda i, j: (i, j)),
        out_shape=jax.ShapeDtypeStruct((t, n), BF16),
        scratch_shapes=[pltpu.VMEM((tm, d), BF16)],
        compiler_params=_cparams(("arbitrary", "arbitrary")),
        name="inproj",
    )(xx, mod, w_in)


def _rope_angles(pos, dim):
    freqs = ROPE_THETA ** (-jnp.arange(0, dim, 2, dtype=F32) / dim)
    return pos.astype(F32)[:, None] * freqs[None, :]


def _rope_tables(seq_len):
    pos = jnp.arange(seq_len, dtype=jnp.int32)
    ang = _rope_angles(pos, RET_DIM)
    c_ret = jnp.concatenate([jnp.cos(ang)] * 2, axis=1)
    s_ret = jnp.concatenate([-jnp.sin(ang), jnp.sin(ang)], axis=1)
    rows = pos // GRID_W
    cols = pos % GRID_W
    ar = _rope_angles(rows, ATT_DIM // 2)
    ac = _rope_angles(cols, ATT_DIM // 2)
    c_att = jnp.concatenate([jnp.cos(ar)] * 2 + [jnp.cos(ac)] * 2, axis=1)
    s_att = jnp.concatenate([-jnp.sin(ar), jnp.sin(ar), -jnp.sin(ac), jnp.sin(ac)], axis=1)
    one = jnp.ones((CTX_LEN, LANES), F32)
    zero = jnp.zeros((CTX_LEN, LANES), F32)
    cat = lambda a, b: jnp.concatenate([a, b], axis=0)
    return cat(one, c_ret), cat(zero, s_ret), cat(one, c_att), cat(zero, s_att)


def _ret_tables(rc):
    lg = jnp.log(1.0 - 2.0 ** (-5.0 - jnp.arange(RET_HEADS, dtype=F32)))
    idx = jnp.arange(rc, dtype=F32)
    diff = idx[:, None] - idx[None, :]
    dec, zeta, xi, gch = [], [], [], []
    for direction, l in enumerate((lg, lg[::-1])):
        if direction == 0:
            dec.append(jnp.where(diff >= 0, jnp.exp(jnp.maximum(diff, 0.0)[None] * l[:, None, None]), 0.0))
            zeta.append(jnp.exp((rc - 1.0 - idx)[None, :] * l[:, None]))
            xi.append(jnp.exp((idx + 1.0)[None, :] * l[:, None]))
        else:
            dec.append(jnp.where(diff <= 0, jnp.exp(jnp.maximum(-diff, 0.0)[None] * l[:, None, None]), 0.0))
            zeta.append(jnp.exp(idx[None, :] * l[:, None]))
            xi.append(jnp.exp((rc - idx)[None, :] * l[:, None]))
        gch.append(jnp.exp(rc * l))
    dec = jnp.concatenate(dec, axis=0)
    bc = lambda a: jnp.broadcast_to(jnp.concatenate(a, axis=0)[:, :, None], (2 * RET_HEADS, rc, LANES))
    gch = jnp.broadcast_to(jnp.concatenate(gch, axis=0)[:, None, None], (2 * RET_HEADS, SUBLANES, LANES))
    return dec, bc(zeta), bc(xi), gch


def _rope_half(x, c, s):
    return x * c + pltpu.roll(x, RET_DIM // 2, axis=1) * s


def _group_norm(y):
    yc = y - jnp.mean(y, axis=-1, keepdims=True)
    return yc * lax.rsqrt(jnp.mean(yc * yc, axis=-1, keepdims=True) + EPS)


def _ret_body(qf, kf, vf, gf, cf, sf, qb, kb, vb, gb, cb, sb, dec_ref, zeta_ref, xi_ref, gch_ref,
              of_ref, ob_ref, state_ref):
    @pl.when(pl.program_id(0) == 0)
    def _():
        state_ref[...] = jnp.zeros_like(state_ref)

    dirs = ((qf, kf, vf, gf, cf, sf, of_ref), (qb, kb, vb, gb, cb, sb, ob_ref))
    for direction, (q_ref, k_ref, v_ref, g_ref, c_ref, s_ref, o_ref) in enumerate(dirs):
        c = c_ref[...]
        s = s_ref[...]
        for h in range(RET_HEADS):
            t = direction * RET_HEADS + h
            sl = slice(h * RET_DIM, (h + 1) * RET_DIM)
            q = _rope_half(q_ref[:, sl].astype(F32), c, s).astype(BF16)
            k = _rope_half(k_ref[:, sl].astype(F32), c, s) * RET_DIM ** -0.5
            v = v_ref[:, sl]
            state = state_ref[t]
            scores = _dot_nt(q, k.astype(BF16)) * dec_ref[t]
            y = _dot(scores.astype(BF16), v) + _dot(q, state.astype(BF16)) * xi_ref[t]
            kv = _dot_tn((k * zeta_ref[t]).astype(BF16), v)
            state_ref[t] = state * gch_ref[t][0:1, :] + kv
            g = g_ref[:, sl].astype(F32)
            o_ref[:, sl] = (_silu(g) * _group_norm(y)).astype(BF16)


def _ret_call(z, c_ret, s_ret, tables):
    t = z.shape[0]
    rc = RET_CHUNK
    nch = t // rc
    nctx = CTX_LEN // rc
    dec, zeta, xi, gch = tables

    def bwd(s):
        return jnp.where(s < nctx, nctx - 1 - s, nch + nctx - 1 - s)

    zf = lambda col: pl.BlockSpec((rc, GROUP_W), lambda s, col=col: (s, col))
    zb = lambda col: pl.BlockSpec((rc, GROUP_W), lambda s, col=col: (bwd(s), col))
    tf = pl.BlockSpec((rc, LANES), lambda s: (s, 0))
    tb = pl.BlockSpec((rc, LANES), lambda s: (bwd(s), 0))
    full = lambda a: pl.BlockSpec(a.shape, lambda s: (0,) * a.ndim)
    out = jax.ShapeDtypeStruct((t, GROUP_W), BF16)
    return pl.pallas_call(
        _ret_body,
        grid=(nch,),
        in_specs=[zf(0), zf(1), zf(2), zf(3), tf, tf, zb(0), zb(1), zb(2), zb(4), tb, tb,
                  full(dec), full(zeta), full(xi), full(gch)],
        out_specs=[pl.BlockSpec((rc, GROUP_W), lambda s: (s, 0)),
                   pl.BlockSpec((rc, GROUP_W), lambda s: (bwd(s), 0))],
        out_shape=[out, out],
        scratch_shapes=[pltpu.VMEM((2 * RET_HEADS, RET_DIM, RET_DIM), F32)],
        compiler_params=_cparams(("arbitrary",)),
        name="retention",
    )(z, z, z, z, c_ret, s_ret, z, z, z, z, c_ret, s_ret, dec, zeta, xi, gch)


S5_BLK = 4
S5_BLK_IN = GROUP_W // S5_BLK
S5_BLK_ST = S5_STATE // S5_BLK
S5_TILE = CTX_LEN
S5_SEG = SUBLANES
S5_STEPS = S5_TILE // S5_SEG
C_A, C_S1, C_S2, C_S4, C_A32 = 0, 2, 4, 6, 8
S5_NCOEF = 10


def _segment_perm():
    p = jnp.arange(S5_TILE)
    src = (p % S5_SEG) * S5_STEPS + p // S5_SEG
    return (src[:, None] == jnp.arange(S5_TILE)[None, :]).astype(BF16)


def _block_diag(w, rows_per_group, cols_per_group):
    gpb = S5_NG // S5_BLK
    w4 = w.reshape(S5_BLK, gpb, rows_per_group, cols_per_group)
    eye = jnp.eye(gpb, dtype=w.dtype)
    out = w4[:, :, :, None, :] * eye[None, :, None, :, None]
    return out.reshape(S5_BLK, gpb * rows_per_group, gpb * cols_per_group)


def _cmul(ar, ai, br, bi):
    return ar * br - ai * bi, ar * bi + ai * br


def _s5_prep_body(are_ref, aim_ref, ls_ref, wbr_ref, wbi_ref, coef_ref, pow_ref, wfr_ref, wfi_ref):
    a_re = are_ref[...]
    a_im = aim_ref[...]
    step = jnp.exp(ls_ref[...])
    mag = jnp.exp(a_re * step)
    ab_re = mag * jnp.cos(a_im * step)
    ab_im = mag * jnp.sin(a_im * step)
    den = a_re * a_re + a_im * a_im
    num_re = ab_re - 1.0
    f_re = (num_re * a_re + ab_im * a_im) / den
    f_im = (ab_im * a_re - num_re * a_im) / den
    row = lax.broadcasted_iota(jnp.int32, (S5_SEG, S5_STATE), 0)
    slab = lambda v: jnp.broadcast_to(v, (S5_SEG, S5_STATE))
    for d in range(2):
        ar, ai = ab_re[d:d + 1], ab_im[d:d + 1]
        pr, pi = [ar], [ai]
        for _ in range(S5_STEPS - 1):
            nr, ni = _cmul(pr[-1], pi[-1], ar, ai)
            pr.append(nr)
            pi.append(ni)
        for t in range(S5_STEPS):
            p = t if d == 0 else S5_STEPS - 1 - t
            pow_ref[d, 0, t:t + 1, :] = pr[p]
            pow_ref[d, 1, t:t + 1, :] = pi[p]
        seg = [(pr[-1], pi[-1])]
        seg.append(_cmul(*seg[-1], *seg[-1]))
        seg.append(_cmul(*seg[-1], *seg[-1]))
        coef_ref[d, C_A] = slab(ar)
        coef_ref[d, C_A + 1] = slab(ai)
        for n, shift in enumerate((1, 2, 4)):
            keep = (row >= shift) if d == 0 else (row < S5_SEG - shift)
            coef_ref[d, C_S1 + 2 * n] = jnp.where(keep, seg[n][0], 0.0)
            coef_ref[d, C_S1 + 2 * n + 1] = jnp.where(keep, seg[n][1], 0.0)
        coef_ref[d, C_A32] = slab(seg[0][0])
        coef_ref[d, C_A32 + 1] = slab(seg[0][1])
        for b in range(S5_BLK):
            cs = slice(b * S5_BLK_ST, (b + 1) * S5_BLK_ST)
            fr = f_re[d:d + 1, cs]
            fi = f_im[d:d + 1, cs]
            wfr_ref[d, b] = (wbr_ref[b] * fr - wbi_ref[b] * fi).astype(BF16)
            wfi_ref[d, b] = (wbr_ref[b] * fi + wbi_ref[b] * fr).astype(BF16)


def _s5_prep_call(a_re, a_im, log_step, b_re, b_im):
    flat = lambda a: a.astype(F32).reshape(2, S5_STATE)
    ls = jnp.repeat(log_step.astype(F32), S5_P, axis=-1)
    wbr = _block_diag(jnp.swapaxes(b_re.astype(F32), 1, 2), S5_GROUP, S5_P)
    wbi = _block_diag(jnp.swapaxes(b_im.astype(F32), 1, 2), S5_GROUP, S5_P)
    wshape = jax.ShapeDtypeStruct((2, S5_BLK, S5_BLK_IN, S5_BLK_ST), BF16)
    return pl.pallas_call(
        _s5_prep_body,
        out_shape=[jax.ShapeDtypeStruct((2, S5_NCOEF, S5_SEG, S5_STATE), F32),
                   jax.ShapeDtypeStruct((2, 2, S5_STEPS, S5_STATE), F32), wshape, wshape],
        compiler_params=pltpu.CompilerParams(vmem_limit_bytes=32 * MIB),
        name="s5_prep",
    )(flat(a_re), flat(a_im), ls, wbr, wbi)


def _s5_scan_body(uf_ref, ub_ref, perm_ref, permt_ref, coef_ref, pow_ref, wfr_ref, wfi_ref, wcr_ref, wci_ref,
                  yf_ref, yb_ref, dr_ref, di_ref, yp_ref, car_ref):
    @pl.when(pl.program_id(0) == 0)
    def _():
        car_ref[...] = jnp.zeros_like(car_ref)

    row = lax.broadcasted_iota(jnp.int32, (S5_SEG, S5_STATE), 0)
    for d, (u_ref, y_ref) in enumerate(((uf_ref, yf_ref), (ub_ref, yb_ref))):
        u = _dot(perm_ref[...], u_ref[...]).astype(BF16)
        for b in range(S5_BLK):
            ub = u[:, b * S5_BLK_IN:(b + 1) * S5_BLK_IN]
            cs = slice(b * S5_BLK_ST, (b + 1) * S5_BLK_ST)
            dr_ref[d, :, cs] = _dot(ub, wfr_ref[d, b])
            di_ref[d, :, cs] = _dot(ub, wfi_ref[d, b])

        def slab_of(n, d=d):
            step = n if d == 0 else S5_STEPS - 1 - n
            return step, pl.ds(pl.multiple_of(step * S5_SEG, S5_SEG), S5_SEG)

        def local(n, state, d=d):
            _, rows = slab_of(n)
            a_re = coef_ref[d, C_A]
            a_im = coef_ref[d, C_A + 1]
            xr = a_re * state[0] - a_im * state[1] + dr_ref[d, rows, :]
            xi = a_re * state[1] + a_im * state[0] + di_ref[d, rows, :]
            dr_ref[d, rows, :] = xr
            di_ref[d, rows, :] = xi
            return xr, xi

        zero = jnp.zeros((S5_SEG, S5_STATE), F32)
        f_re, f_im = lax.fori_loop(0, S5_STEPS, local, (zero, zero), unroll=True)

        edge = 0 if d == 0 else S5_SEG - 1
        one = 1 if d == 0 else S5_SEG - 1
        g_re = jnp.where(row == edge, car_ref[d, 0:1, :], pltpu.roll(f_re, one, axis=0))
        g_im = jnp.where(row == edge, car_ref[d, 1:2, :], pltpu.roll(f_im, one, axis=0))
        for n, shift in enumerate((1, 2, 4)):
            amt = shift if d == 0 else S5_SEG - shift
            sr = pltpu.roll(g_re, amt, axis=0)
            si = pltpu.roll(g_im, amt, axis=0)
            cr = coef_ref[d, C_S1 + 2 * n]
            ci = coef_ref[d, C_S1 + 2 * n + 1]
            g_re, g_im = g_re + cr * sr - ci * si, g_im + cr * si + ci * sr
        out_re = f_re + coef_ref[d, C_A32] * g_re - coef_ref[d, C_A32 + 1] * g_im
        out_im = f_im + coef_ref[d, C_A32] * g_im + coef_ref[d, C_A32 + 1] * g_re
        last = S5_SEG - 1 - edge
        car_ref[d, 0:1, :] = out_re[last:last + 1, :]
        car_ref[d, 1:2, :] = out_im[last:last + 1, :]

        def fix(n, carry, d=d, g_re=g_re, g_im=g_im):
            step, rows = slab_of(n)
            pr = pow_ref[d, 0, pl.ds(step, 1), :]
            pi = pow_ref[d, 1, pl.ds(step, 1), :]
            dr_ref[d, rows, :] = dr_ref[d, rows, :] + pr * g_re - pi * g_im
            di_ref[d, rows, :] = di_ref[d, rows, :] + pr * g_im + pi * g_re
            return carry

        lax.fori_loop(0, S5_STEPS, fix, 0, unroll=True)
        for b in range(S5_BLK):
            cs = slice(b * S5_BLK_ST, (b + 1) * S5_BLK_ST)
            yp_ref[:, b * S5_BLK_IN:(b + 1) * S5_BLK_IN] = (
                _dot(dr_ref[d, :, cs].astype(BF16), wcr_ref[b]) - _dot(di_ref[d, :, cs].astype(BF16), wci_ref[b]))
        yp = yp_ref[...]
        hi = yp.astype(BF16)
        lo = (yp - hi.astype(F32)).astype(BF16)
        y_ref[...] = _dot(permt_ref[...], hi) + _dot(permt_ref[...], lo)


def _s5_scan_call(z, coef, pows, wfr, wfi, wcr, wci):
    t = z.shape[0]
    tm = S5_TILE
    nt = t // tm
    nctx = CTX_LEN // tm
    ucol = S5_OFF // GROUP_W
    perm = _segment_perm()
    permt = perm.T

    def bwd(s):
        return jnp.where(s < nctx, nctx - 1 - s, nt + nctx - 1 - s)

    full = lambda a: pl.BlockSpec(a.shape, lambda s: (0,) * a.ndim)
    out = jax.ShapeDtypeStruct((t, GROUP_W), F32)
    return pl.pallas_call(
        _s5_scan_body,
        grid=(nt,),
        in_specs=[pl.BlockSpec((tm, GROUP_W), lambda s: (s, ucol)),
                  pl.BlockSpec((tm, GROUP_W), lambda s: (bwd(s), ucol)),
                  full(perm), full(permt), full(coef), full(pows), full(wfr), full(wfi), full(wcr), full(wci)],
        out_specs=[pl.BlockSpec((tm, GROUP_W), lambda s: (s, 0)),
                   pl.BlockSpec((tm, GROUP_W), lambda s: (bwd(s), 0))],
        out_shape=[out, out],
        scratch_shapes=[pltpu.VMEM((2, tm, S5_STATE), F32), pltpu.VMEM((2, tm, S5_STATE), F32),
                        pltpu.VMEM((tm, GROUP_W), F32), pltpu.VMEM((2, SUBLANES, S5_STATE), F32)],
        compiler_params=_cparams(("arbitrary",)),
        name="s5_scan",
    )(z, z, perm, permt, coef, pows, wfr, wfi, wcr, wci)


def _gelu_tanh(x):
    return 0.5 * x * (1.0 + jnp.tanh(math.sqrt(2.0 / math.pi) * (x + 0.044715 * (x * x * x))))


def _s5_finish_body(yf_ref, yb_ref, u_ref, d_ref, w_ref, b_ref, o_ref):
    y = yf_ref[...] + yb_ref[...] + u_ref[...].astype(F32) * d_ref[...]
    y = _gelu_tanh(y)
    o_ref[...] = (y * jax.nn.sigmoid(_dot(y.astype(BF16), w_ref[...]) + b_ref[...])).astype(BF16)


def _s5_finish_call(yf, yb, z, d_skip, w_glu, b_glu):
    t = z.shape[0]
    tm = _pick(t, (768, 512, 256))
    ucol = S5_OFF // GROUP_W
    row = pl.BlockSpec((tm, GROUP_W), lambda i: (i, 0))
    vec = pl.BlockSpec((1, GROUP_W), lambda i: (0, 0))
    return pl.pallas_call(
        _s5_finish_body,
        grid=(t // tm,),
        in_specs=[row, row, pl.BlockSpec((tm, GROUP_W), lambda i: (i, ucol)), vec,
                  pl.BlockSpec((GROUP_W, GROUP_W), lambda i: (0, 0)), vec],
        out_specs=row,
        out_shape=jax.ShapeDtypeStruct((t, GROUP_W), BF16),
        compiler_params=_cparams(("arbitrary",)),
        name="s5_finish",
    )(yf, yb, z, d_skip.astype(F32).reshape(1, GROUP_W), w_glu.astype(BF16),
      b_glu.astype(F32).reshape(1, GROUP_W))


def _head_rms_rope(x, w, c, s, lane):
    xn = x * lax.rsqrt(jnp.mean(x * x, axis=-1, keepdims=True) + EPS) * w
    quarter = ATT_DIM // 4
    first = (lane % (2 * quarter)) < quarter
    partner = jnp.where(first, pltpu.roll(xn, ATT_DIM - quarter, axis=1), pltpu.roll(xn, quarter, axis=1))
    return xn * c + partner * s


LOG2E = math.log2(math.e)
ATT_KEY_CHUNK_PREFS = (768, 512, 256)


def _att_prep_body(q_ref, k_ref, v_ref, c_ref, s_ref, qw_ref, kw_ref, qo_ref, ko_ref, vo_ref):
    c = c_ref[...]
    s = s_ref[...]
    lane = lax.broadcasted_iota(jnp.int32, c.shape, 1)
    for h in range(ATT_HEADS):
        sl = slice(h * ATT_DIM, (h + 1) * ATT_DIM)
        q = _head_rms_rope(q_ref[:, sl].astype(F32), qw_ref[...], c, s, lane)
        qo_ref[sl, :] = (q * (ATT_DIM ** -0.5 * LOG2E)).T.astype(BF16)
    for h in range(ATT_KV_HEADS):
        sl = slice(h * ATT_DIM, (h + 1) * ATT_DIM)
        ko_ref[:, sl] = _head_rms_rope(k_ref[:, sl].astype(F32), kw_ref[...], c, s, lane).astype(BF16)
        vo_ref[h, 0] = v_ref[:, sl].astype(F32).T.astype(BF16)


def _att_prep_call(z, c_att, s_att, q_norm_w, k_norm_w):
    t = z.shape[0]
    tm = _pick(t, ATT_KEY_CHUNK_PREFS)
    qw, kw = ATT_HEADS * ATT_DIM, ATT_KV_HEADS * ATT_DIM
    tab = pl.BlockSpec((tm, LANES), lambda i: (i, 0))
    vec = pl.BlockSpec((1, ATT_DIM), lambda i: (0, 0))
    return pl.pallas_call(
        _att_prep_body,
        grid=(t // tm,),
        in_specs=[pl.BlockSpec((tm, qw), lambda i: (i, ATT_OFF // qw)),
                  pl.BlockSpec((tm, kw), lambda i: (i, ATT_K_OFF // kw)),
                  pl.BlockSpec((tm, kw), lambda i: (i, ATT_V_OFF // kw)), tab, tab, vec, vec],
        out_specs=[pl.BlockSpec((qw, tm), lambda i: (0, i)), pl.BlockSpec((tm, kw), lambda i: (i, 0)),
                   pl.BlockSpec((ATT_KV_HEADS, 1, ATT_DIM, tm), lambda i: (0, i, 0, 0))],
        out_shape=[jax.ShapeDtypeStruct((qw, t), BF16), jax.ShapeDtypeStruct((t, kw), BF16),
                   jax.ShapeDtypeStruct((ATT_KV_HEADS, t // tm, ATT_DIM, tm), BF16)],
        compiler_params=_cparams(("arbitrary",)),
        name="att_prep",
    )(z, z, z, c_att, s_att, q_norm_w.astype(F32).reshape(1, ATT_DIM), k_norm_w.astype(F32).reshape(1, ATT_DIM))


ATT_SUB = 256


ST_M, ST_L, ST_BUF = 0, 1, 2


def _flash_body(q0_ref, q1_ref, k_ref, v_ref, o_ref, sa_ref, sb_ref, p_ref, st_ref, acc_ref, *, tq, tk, n_lat):
    q_refs = (q0_ref, q1_ref)
    grp = len(q_refs)
    st_ref[...] = jnp.zeros_like(st_ref)
    st_ref[ST_M:ST_M + 1, :] = jnp.full((1, grp * tq), NEG_BIG, F32)
    acc_ref[...] = jnp.zeros_like(acc_ref)
    bufs = ((sa_ref, ST_BUF), (sb_ref, ST_BUF + 2))

    def qk_pass(c, keys, buf):
        s_ref, row = bufs[buf]
        m_old = st_ref[ST_M:ST_M + 1, :]
        m_new = []
        for g in range(grp):
            tile_max = []
            for j in range(keys // ATT_SUB):
                r0 = c * tk + j * ATT_SUB
                r0 = r0 if isinstance(r0, int) else pl.multiple_of(r0, ATT_SUB)
                s = _dot(k_ref[pl.ds(r0, ATT_SUB), :], q_refs[g][...])
                s_ref[g, j * ATT_SUB:(j + 1) * ATT_SUB, :] = s
                tile_max.append(jnp.max(s.reshape(ATT_SUB // SUBLANES, SUBLANES, tq), axis=0))
            m8 = functools.reduce(jnp.maximum, tile_max)
            m_new.append(jnp.maximum(m_old[:, g * tq:(g + 1) * tq], jnp.max(m8, axis=0, keepdims=True)))
        m_cat = jnp.concatenate(m_new, axis=1)
        st_ref[ST_M:ST_M + 1, :] = m_cat
        st_ref[row:row + 1, :] = m_cat
        st_ref[row + 1:row + 2, :] = jnp.exp2(m_old - m_cat)

    def pv_pass(c, keys, buf):
        s_ref, row = bufs[buf]
        m_c = st_ref[row:row + 1, :]
        alpha = st_ref[row + 1:row + 2, :]
        l_add = []
        for g in range(grp):
            mg = m_c[:, g * tq:(g + 1) * tq]
            tile_sum = []
            for j in range(keys // ATT_SUB):
                rows = slice(j * ATT_SUB, (j + 1) * ATT_SUB)
                p = jnp.exp2(s_ref[g, rows, :] - mg)
                tile_sum.append(jnp.sum(p.reshape(ATT_SUB // SUBLANES, SUBLANES, tq), axis=0))
                p_ref[rows, g * tq:(g + 1) * tq] = p.astype(BF16)
            l_add.append(jnp.sum(functools.reduce(jnp.add, tile_sum), axis=0, keepdims=True))
        st_ref[ST_L:ST_L + 1, :] = alpha * st_ref[ST_L:ST_L + 1, :] + jnp.concatenate(l_add, axis=1)
        if keys == tk:
            pv = _dot(v_ref[0, c], p_ref[...])
        else:
            pv = _dot(v_ref[0, 0, :, 0:keys], p_ref[0:keys, :])
        acc_ref[...] = acc_ref[...] * alpha + pv

    is_ctx = pl.program_id(1) < CTX_LEN // tq

    @pl.when(is_ctx)
    def _():
        qk_pass(0, CTX_LEN, 0)
        pv_pass(0, CTX_LEN, 0)

    @pl.when(jnp.logical_not(is_ctx))
    def _():
        qk_pass(0, tk, 0)

        def pair(i, carry):
            c = 2 * i
            pv_pass(c, tk, 0)
            qk_pass(c + 1, tk, 1)
            pv_pass(c + 1, tk, 1)
            qk_pass(c + 2, tk, 0)
            return carry

        n_pairs = (n_lat - 1) // 2
        lax.fori_loop(0, n_pairs, pair, 0)
        pv_pass(2 * n_pairs, tk, 0)
        if n_lat % 2 == 0:
            qk_pass(n_lat - 1, tk, 1)
            pv_pass(n_lat - 1, tk, 1)

    o = acc_ref[...] / st_ref[ST_L:ST_L + 1, :]
    for g in range(grp):
        o_ref[:, g * ATT_DIM:(g + 1) * ATT_DIM] = o[:, g * tq:(g + 1) * tq].T.astype(BF16)


def _flash_call(qt, kn, vt):
    t = kn.shape[0]
    tq = CTX_LEN
    n_chunks, tk = vt.shape[1], vt.shape[3]
    grp = ATT_HEADS // ATT_KV_HEADS
    assert grp == 2
    qspec = lambda g: pl.BlockSpec((ATT_DIM, tq), lambda h, i, g=g: (grp * h + g, i))
    return pl.pallas_call(
        functools.partial(_flash_body, tq=tq, tk=tk, n_lat=n_chunks),
        grid=(ATT_KV_HEADS, t // tq),
        in_specs=[qspec(0), qspec(1),
                  pl.BlockSpec((t, ATT_DIM), lambda h, i: (0, h)),
                  pl.BlockSpec((1, n_chunks, ATT_DIM, tk), lambda h, i: (h, 0, 0, 0))],
        out_specs=pl.BlockSpec((tq, grp * ATT_DIM), lambda h, i: (i, h)),
        out_shape=jax.ShapeDtypeStruct((t, ATT_HEADS * ATT_DIM), BF16),
        scratch_shapes=[pltpu.VMEM((grp, tk, tq), F32), pltpu.VMEM((grp, tk, tq), F32),
                        pltpu.VMEM((tk, grp * tq), BF16), pltpu.VMEM((SUBLANES, grp * tq), F32),
                        pltpu.VMEM((ATT_DIM, grp * tq), F32)],
        compiler_params=_cparams(("arbitrary", "arbitrary")),
        name="flash_attention",
    )(qt, qt, kn, vt)


def _conv_body(b_ref, c_ref, h_ref, w_ref, o_ref):
    t = b_ref.shape[0]
    p = c_ref[...].astype(F32) * h_ref[...].astype(F32)
    row = lax.broadcasted_iota(jnp.int32, p.shape, 0)
    seq_first = (row == 0) | (row == CTX_LEN)
    seq_last = (row == CTX_LEN - 1) | (row == t - 1)
    prev = jnp.where(seq_first, 0.0, pltpu.roll(p, 1, axis=0))
    nxt = jnp.where(seq_last, 0.0, pltpu.roll(p, t - 1, axis=0))
    w = w_ref[...]
    y = w[0:1, :] * prev + w[1:2, :] * p + w[2:3, :] * nxt
    o_ref[...] = (b_ref[...].astype(F32) * y).astype(BF16)


def _conv_call(z, conv_w):
    t = z.shape[0]
    col0 = CONV_OFF // LANES
    nblk = GROUP_W // LANES
    zc = lambda k: pl.BlockSpec((t, LANES), lambda j, k=k: (0, col0 + k * nblk + j))
    return pl.pallas_call(
        _conv_body,
        grid=(nblk,),
        in_specs=[zc(0), zc(1), zc(2), pl.BlockSpec((CONV_W, LANES), lambda j: (0, j))],
        out_specs=pl.BlockSpec((t, LANES), lambda j: (0, j)),
        out_shape=jax.ShapeDtypeStruct((t, GROUP_W), BF16),
        compiler_params=_cparams(("arbitrary",)),
        name="gated_conv",
    )(z, z, z, conv_w.astype(F32))


ROUTE_IDX, ROUTE_GATE, ROUTE_RANK = 0, TOP_K, 2 * TOP_K


def _outproj_body(rf_ref, rb_ref, s_ref, a_ref, v_ref, x_ref, mod_ref, w_ref, wr_ref, br_ref, tri_ref,
                  xo_ref, fo_ref, ro_ref, cnt_ref, count_ref, *, tm):
    i = pl.program_id(0)

    @pl.when(i == 0)
    def _():
        count_ref[...] = jnp.zeros_like(count_ref)

    r = (rf_ref[...].astype(F32) + rb_ref[...].astype(F32)).astype(BF16)
    g = GROUP_W
    mix = (_dot(r, w_ref[0:g, :]) + _dot(s_ref[...], w_ref[g:2 * g, :])
           + _dot(a_ref[...], w_ref[2 * g:3 * g, :]) + _dot(v_ref[...], w_ref[3 * g:4 * g, :]))
    xn = x_ref[...] + _row_mod(mod_ref, i * tm, tm, 2) * mix
    xo_ref[...] = xn
    f = xn * lax.rsqrt(jnp.mean(xn * xn, axis=-1, keepdims=True) + EPS)
    f = f * (1.0 + _row_mod(mod_ref, i * tm, tm, 4)) + _row_mod(mod_ref, i * tm, tm, 3)
    fo_ref[...] = f

    fh = f.astype(BF16)
    fl = (f - fh.astype(F32)).astype(BF16)
    d = f.shape[1]
    logits = (_dot(fh, wr_ref[0:d, :]) + _dot(fl, wr_ref[0:d, :]) + _dot(fh, wr_ref[d:2 * d, :])
              + br_ref[...])
    lane = lax.broadcasted_iota(jnp.int32, logits.shape, 1).astype(F32)
    vals = logits
    top_v, top_i, hits = [], [], []
    for _ in range(TOP_K):
        mk = jnp.max(vals, axis=1, keepdims=True)
        ik = jnp.min(jnp.where(vals == mk, lane, float(LANES)), axis=1, keepdims=True)
        hit = lane == ik
        top_v.append(mk)
        top_i.append(ik)
        hits.append(hit)
        vals = jnp.where(hit, 2.0 * NEG_BIG, vals)
    ex = [jnp.exp(v - top_v[0]) for v in top_v]
    den = ex[0] + ex[1] + ex[2] + ex[3]
    member = jnp.zeros_like(logits)
    for hit in hits:
        member = jnp.where(hit, 1.0, member)
    ahead = _dot(tri_ref[...], member.astype(BF16)) + count_ref[0:1, :]
    record = jnp.zeros_like(logits)
    for k in range(TOP_K):
        rank = jnp.sum(jnp.where(hits[k], ahead, 0.0), axis=1, keepdims=True)
        record = jnp.where(lane == float(ROUTE_IDX + k), top_i[k], record)
        record = jnp.where(lane == float(ROUTE_GATE + k), ex[k] / den, record)
        record = jnp.where(lane == float(ROUTE_RANK + k), rank, record)
    ro_ref[...] = record
    count_ref[...] = count_ref[...] + jnp.sum(member, axis=0, keepdims=True)
    cnt_ref[...] = count_ref[...]


def _outproj_call(rf, rb, s5, att, cv, xx, mod, w_out, w_router, b_router):
    t, d = xx.shape
    tm = CTX_LEN
    wr = jnp.zeros((d, LANES), F32).at[:, :N_EXPERTS].set(w_router.astype(F32))
    wr_hi = wr.astype(BF16)
    wr_lo = (wr - wr_hi.astype(F32)).astype(BF16)
    wr2 = jnp.concatenate([wr_hi, wr_lo], axis=0)
    br = jnp.full((1, LANES), NEG_BIG, F32).at[0, :N_EXPERTS].set(b_router.astype(F32))
    tri = jnp.tri(tm, k=-1, dtype=BF16)
    mix = pl.BlockSpec((tm, GROUP_W), lambda i: (i, 0))
    row = pl.BlockSpec((tm, d), lambda i: (i, 0))
    full = lambda a: pl.BlockSpec(a.shape, lambda i: (0,) * a.ndim)
    rec = pl.BlockSpec((tm, LANES), lambda i: (i, 0))
    return pl.pallas_call(
        functools.partial(_outproj_body, tm=tm),
        grid=(t // tm,),
        in_specs=[mix, mix, mix, mix, mix, row, full(mod), full(w_out), full(wr2), full(br), full(tri)],
        out_specs=[row, row, rec, pl.BlockSpec((SUBLANES, LANES), lambda i: (0, 0))],
        out_shape=[jax.ShapeDtypeStruct((t, d), F32), jax.ShapeDtypeStruct((t, d), F32),
                   jax.ShapeDtypeStruct((t, LANES), F32), jax.ShapeDtypeStruct((SUBLANES, LANES), F32)],
        scratch_shapes=[pltpu.VMEM((SUBLANES, LANES), F32)],
        compiler_params=_cparams(("arbitrary",)),
        name="outproj_route",
    )(rf, rb, s5, att, cv, xx, mod, w_out, wr2, br, tri)


MXU_TILE = 256
WPREP_ROWS = 512


def _deinterleave_matrix():
    src = jnp.arange(MXU_TILE)
    dst = jnp.where(src % 2 == 0, src // 2, MXU_TILE // 2 + src // 2)
    return (dst[:, None] == jnp.arange(MXU_TILE)[None, :]).astype(BF16)


def _gate_up_prep_body(w_ref, p_ref, o_ref):
    half = MXU_TILE // 2
    for c in range(2 * D_EXPERT // MXU_TILE):
        blk = w_ref[0, 0, :, c * MXU_TILE:(c + 1) * MXU_TILE].astype(BF16)
        r = _dot(blk, p_ref[...])
        o_ref[0, :, c * half:(c + 1) * half] = r[:, :half].astype(BF16)
        o_ref[0, :, D_EXPERT + c * half:D_EXPERT + (c + 1) * half] = r[:, half:].astype(BF16)


def _gate_up_prep_call(w_gate_up, layer):
    _, n_e, d, n = w_gate_up.shape
    perm = _deinterleave_matrix()
    return pl.pallas_call(
        _gate_up_prep_body,
        grid=(n_e, d // WPREP_ROWS),
        in_specs=[pl.BlockSpec((1, 1, WPREP_ROWS, n), lambda e, r: (layer, e, r, 0)),
                  pl.BlockSpec(perm.shape, lambda e, r: (0, 0))],
        out_specs=pl.BlockSpec((1, WPREP_ROWS, n), lambda e, r: (e, r, 0)),
        out_shape=jax.ShapeDtypeStruct((n_e, d, n), BF16),
        compiler_params=_cparams(("arbitrary", "arbitrary")),
        name="gate_up_prep",
    )(w_gate_up, perm)


SLOT_BLOCK = 256
GATHER_UNROLL = 8


def _row_copy(src_hbm, row, dst, dst_row, sem):
    return pltpu.make_async_copy(src_hbm.at[pl.ds(row, 1), :], dst.at[pl.ds(dst_row, 1), :], sem)


def _moe_body(tok_ref, be_ref, nu_ref, f_hbm, wgu_ref, wd_ref, bgu_ref, bd_ref, o_ref, xbuf, sem, *, n_blocks):
    b = pl.program_id(0)
    n_used = nu_ref[0]

    def issue_loop(blk, slot):
        def some(n, carry):
            for u in range(GATHER_UNROLL):
                r = n * GATHER_UNROLL + u
                _row_copy(f_hbm, tok_ref[blk * SLOT_BLOCK + r], xbuf.at[slot], r, sem.at[slot]).start()
            return carry
        lax.fori_loop(0, SLOT_BLOCK // GATHER_UNROLL, some, 0)

    def issue_inline(blk, slot):
        for r in range(SLOT_BLOCK):
            _row_copy(f_hbm, tok_ref[blk * SLOT_BLOCK + r], xbuf.at[slot], r, sem.at[slot]).start()

    def wait(slot):
        pltpu.make_async_copy(f_hbm.at[pl.ds(0, SLOT_BLOCK), :], xbuf.at[slot], sem.at[slot]).wait()

    slot = b % 2

    @pl.when((b == 0) & (n_used > 0))
    def _():
        issue_loop(0, 0)

    @pl.when(b < n_used)
    def _():
        wait(slot)
        issue_inline(jnp.minimum(b + 1, n_blocks - 1), 1 - slot)
        x = xbuf[slot].astype(BF16)
        gu = _dot(x, wgu_ref[0]) + bgu_ref[0]
        gate = jnp.minimum(gu[:, :D_EXPERT], SWIGLU_LIMIT)
        up = jnp.clip(gu[:, D_EXPERT:], -SWIGLU_LIMIT, SWIGLU_LIMIT)
        act = (up + 1.0) * gate * jax.nn.sigmoid(gate * SWIGLU_ALPHA)
        o_ref[...] = _dot(act.astype(BF16), wd_ref[0, 0].astype(BF16)) + bd_ref[0]

        @pl.when(b + 1 >= n_used)
        def _():
            wait(1 - slot)

    @pl.when(b >= n_used)
    def _():
        o_ref[...] = jnp.zeros_like(o_ref)


def _moe_call(slot_tok, block_e, n_used, f, wgu, w_down, layer, bgu, bd):
    n_blocks = block_e.shape[0]
    d = f.shape[1]
    e_spec = lambda shape: pl.BlockSpec((1,) + shape, lambda b, tok, be, nu: (be[b], 0, 0))
    grid_spec = pltpu.PrefetchScalarGridSpec(
        num_scalar_prefetch=3,
        grid=(n_blocks,),
        in_specs=[pl.BlockSpec(memory_space=pl.ANY),
                  e_spec((d, 2 * D_EXPERT)),
                  pl.BlockSpec((1, 1, D_EXPERT, d), lambda b, tok, be, nu: (layer, be[b], 0, 0)),
                  e_spec((1, 2 * D_EXPERT)), e_spec((1, d))],
        out_specs=pl.BlockSpec((SLOT_BLOCK, d), lambda b, tok, be, nu: (b, 0)),
        scratch_shapes=[pltpu.VMEM((2, SLOT_BLOCK, d), F32), pltpu.SemaphoreType.DMA((2,))],
    )
    return pl.pallas_call(
        functools.partial(_moe_body, n_blocks=n_blocks),
        grid_spec=grid_spec,
        out_shape=jax.ShapeDtypeStruct((n_blocks * SLOT_BLOCK, d), F32),
        compiler_params=_cparams(("arbitrary",), vmem_mib=56),
        name="moe_experts",
    )(slot_tok, block_e, n_used, f, wgu, w_down, bgu, bd)


COMBINE_TOKENS = 64


def _combine_body(dest_ref, y_hbm, gate_ref, x_ref, mod_ref, o_ref, buf, sem):
    i = pl.program_id(0)
    n = pl.num_programs(0)
    tt = COMBINE_TOKENS

    def issue(tile, slot):
        def one(r, carry):
            for k in range(TOP_K):
                _row_copy(y_hbm, dest_ref[(tile * tt + r) * TOP_K + k], buf.at[slot, k], r, sem.at[slot]).start()
            return carry
        lax.fori_loop(0, tt, one, 0)

    def wait(slot):
        def one(r, carry):
            for k in range(TOP_K):
                _row_copy(y_hbm, 0, buf.at[slot, k], r, sem.at[slot]).wait()
            return carry
        lax.fori_loop(0, tt, one, 0)

    slot = i % 2

    @pl.when(i == 0)
    def _():
        issue(0, 0)

    wait(slot)

    @pl.when(i + 1 < n)
    def _():
        issue(i + 1, 1 - slot)

    gates = gate_ref[...]
    y = buf[slot, 0] * gates[:, 0:1]
    for k in range(1, TOP_K):
        y = y + buf[slot, k] * gates[:, k:k + 1]
    o_ref[...] = x_ref[...] + mod_ref[0, 5:6, :] * y


def _combine_call(dest, yb, gates, xnew, mod, latent_only):
    t, d = xnew.shape
    tt = COMBINE_TOKENS
    nctx = CTX_LEN // tt
    if latent_only:
        out_rows = t - CTX_LEN
        out_map = lambda i, dest: (jnp.maximum(i - nctx, 0), 0)
    else:
        out_rows = t
        out_map = lambda i, dest: (i, 0)
    grid_spec = pltpu.PrefetchScalarGridSpec(
        num_scalar_prefetch=1,
        grid=(t // tt,),
        in_specs=[pl.BlockSpec(memory_space=pl.ANY),
                  pl.BlockSpec((tt, TOP_K), lambda i, dest: (i, 0)),
                  pl.BlockSpec((tt, d), lambda i, dest: (i, 0)),
                  pl.BlockSpec((1, 6, d), lambda i, dest: (jnp.where(i < nctx, 1, 0), 0, 0))],
        out_specs=pl.BlockSpec((tt, d), out_map),
        scratch_shapes=[pltpu.VMEM((2, TOP_K, tt, d), F32), pltpu.SemaphoreType.DMA((2,))],
    )
    return pl.pallas_call(
        _combine_body,
        grid_spec=grid_spec,
        out_shape=jax.ShapeDtypeStruct((out_rows, d), F32),
        compiler_params=_cparams(("arbitrary",)),
        name="moe_combine",
    )(dest, yb, gates, xnew, mod)


def _routing_tables(record, counts, t):
    n_blocks = -(-(t * TOP_K + N_EXPERTS * (SLOT_BLOCK - 1)) // SLOT_BLOCK)
    top_i = record[:, ROUTE_IDX:ROUTE_IDX + TOP_K].astype(jnp.int32)
    gates = record[:, ROUTE_GATE:ROUTE_GATE + TOP_K]
    rank = record[:, ROUTE_RANK:ROUTE_RANK + TOP_K].astype(jnp.int32)
    cnt = counts[0, :N_EXPERTS].astype(jnp.int32)
    padded = (cnt + SLOT_BLOCK - 1) // SLOT_BLOCK * SLOT_BLOCK
    pend = jnp.cumsum(padded)
    pstart = pend - padded
    dest = (pstart[top_i] + rank).reshape(-1)
    flat_tok = jnp.repeat(jnp.arange(t, dtype=jnp.int32), TOP_K)
    slot_tok = jnp.zeros((n_blocks * SLOT_BLOCK,), jnp.int32).at[dest].set(flat_tok)
    block_start = jnp.arange(n_blocks, dtype=jnp.int32) * SLOT_BLOCK
    block_e = jnp.minimum(jnp.sum((pend[None, :] <= block_start[:, None]).astype(jnp.int32), axis=1),
                          N_EXPERTS - 1).astype(jnp.int32)
    n_used = (pend[-1:] // SLOT_BLOCK).astype(jnp.int32)
    return dest.astype(jnp.int32), gates, slot_tok, block_e, n_used


def kernel(x, c, ctx, c_ctx, w_mod, b_mod, w_in, w_out, s5_a_re, s5_a_im, s5_log_step, s5_b_re, s5_b_im,
           s5_c_re, s5_c_im, s5_d, s5_w_glu, s5_b_glu, q_norm_w, k_norm_w, conv_w, w_router, b_router,
           w_gate_up, b_gate_up, w_down, b_down):
    bsz, seq_len, d = x.shape
    assert bsz == 1 and d == D_MODEL and ctx.shape[1] == CTX_LEN
    t = CTX_LEN + seq_len
    xx = jnp.concatenate([ctx[0], x[0]], axis=0).astype(F32)

    cc = jnp.zeros((SUBLANES, d), F32).at[0].set(c[0]).at[1].set(c_ctx)
    mods = _mod_call(cc, w_mod.astype(F32), b_mod.astype(F32))[:, :2].reshape(DEPTH, 2, 6, d)
    c_ret, s_ret, c_att, s_att = _rope_tables(seq_len)
    ret_tables = _ret_tables(RET_CHUNK)

    for l in range(DEPTH):
        mod = mods[l]
        z = _inproj_call(xx, mod, w_in[l].astype(BF16))
        rf, rb = _ret_call(z, c_ret, s_ret, ret_tables)
        coef, pows, wfr, wfi = _s5_prep_call(s5_a_re[l], s5_a_im[l], s5_log_step[l], s5_b_re[l], s5_b_im[l])
        wcr = _block_diag(jnp.swapaxes(s5_c_re[l].astype(F32), 1, 2), S5_P, S5_GROUP).astype(BF16)
        wci = _block_diag(jnp.swapaxes(s5_c_im[l].astype(F32), 1, 2), S5_P, S5_GROUP).astype(BF16)
        yf, yb = _s5_scan_call(z, coef, pows, wfr, wfi, wcr, wci)
        s5 = _s5_finish_call(yf, yb, z, s5_d[l], s5_w_glu[l], s5_b_glu[l])
        qt, kn, vt = _att_prep_call(z, c_att, s_att, q_norm_w[l], k_norm_w[l])
        att = _flash_call(qt, kn, vt)
        cv = _conv_call(z, conv_w[l])
        xnew, f, record, counts = _outproj_call(rf, rb, s5, att, cv, xx, mod, w_out[l].astype(BF16),
                                                w_router[l], b_router[l])
        dest, gates, slot_tok, block_e, n_used = _routing_tables(record, counts, t)
        wgu = _gate_up_prep_call(w_gate_up, l)
        bgu = jnp.concatenate([b_gate_up[l][:, 0::2], b_gate_up[l][:, 1::2]], axis=1)[:, None, :].astype(F32)
        y_slots = _moe_call(slot_tok, block_e, n_used, f, wgu, w_down, l, bgu, b_down[l][:, None, :].astype(F32))
        xx = _combine_call(dest, y_slots, gates, xnew, mod, latent_only=(l == DEPTH - 1))

    return xx[None].astype(x.dtype)
```
